```python
import jax, jax.numpy as jnp
from jax import lax
import numpy as np

D_MODEL = 1024
BATCH = 16
SEQ = 2048
DEPTH = 1

RET_HEADS = 4
RET_QK_DIM = 128
RET_V_DIM = 256
RET_CHUNK = 128
ROPE_BASE = 10000.0
MOBA_HEADS = 8
MOBA_HEAD_DIM = 128
MOBA_BLOCK = 256
MOBA_TOPK = 3
MOBA_Q_CHUNK = 16
D_FF = 2816
CONV_WIDTH = 3
NORM_EPS = 1e-6

RET_QK = RET_HEADS * RET_QK_DIM
RET_V = RET_HEADS * RET_V_DIM
MOBA_W = MOBA_HEADS * MOBA_HEAD_DIM
IN_SPLITS = (RET_QK, RET_QK, RET_V, RET_V, MOBA_W, MOBA_W, MOBA_W)
D_IN = RET_QK * 2 + RET_V * 2 + MOBA_W * 3

kernel_name = "hybrid_retention_moba_convffn"


def rms_norm(x, gain):
    xf = x.astype(jnp.float32)
    inv = lax.rsqrt(jnp.mean(xf * xf, axis=-1, keepdims=True) + NORM_EPS)
    return (xf * inv).astype(x.dtype) * gain


def rotary(x, pos):
    d = x.shape[-1]
    half = d // 2
    inv_freq = ROPE_BASE ** (-jnp.arange(half, dtype=jnp.float32) / half)
    ang = pos.astype(jnp.float32)[:, None] * inv_freq[None, :]
    cos = jnp.cos(ang).astype(x.dtype)
    sin = jnp.sin(ang).astype(x.dtype)
    x1, x2 = x[..., :half], x[..., half:]
    return jnp.concatenate([x1 * cos - x2 * sin, x1 * sin + x2 * cos], axis=-1)


def retention_chunkwise(q, k, v):
    B, H, S, dk = q.shape
    dv = v.shape[-1]
    C = RET_CHUNK
    n = S // C
    dt = q.dtype
    log_gamma = jnp.log1p(-jnp.power(2.0, -5.0 - jnp.arange(H, dtype=jnp.float32)))
    pos = jnp.arange(C, dtype=jnp.float32)
    diff = pos[:, None] - pos[None, :]
    decay_in = jnp.where(diff >= 0, jnp.exp(log_gamma[:, None, None] * jnp.maximum(diff, 0.0)), 0.0).astype(dt)
    zeta = jnp.exp(log_gamma[:, None] * (C - 1 - pos)[None, :]).astype(dt)
    xi = jnp.exp(log_gamma[:, None] * (pos + 1)[None, :]).astype(dt)
    gamma_c = jnp.exp(log_gamma * C).astype(dt)
    qc = q.reshape(B, H, n, C, dk)
    kc = k.reshape(B, H, n, C, dk)
    vc = v.reshape(B, H, n, C, dv)
    scores = jnp.einsum('bhncd,bhnmd->bhncm', qc, kc) * decay_in[None, :, None]
    o_inner = jnp.einsum('bhncm,bhnme->bhnce', scores, vc)
    kv = jnp.einsum('bhnmd,hm,bhnme->bhnde', kc, zeta, vc)

    def step(R, kv_n):
        return gamma_c[None, :, None, None] * R + kv_n, R

    _, r_prev = lax.scan(step, jnp.zeros((B, H, dk, dv), dt), jnp.moveaxis(kv, 2, 0))
    r_prev = jnp.moveaxis(r_prev, 0, 2)
    o_cross = jnp.einsum('bhncd,hc,bhnde->bhnce', qc, xi, r_prev)
    return (o_inner + o_cross).reshape(B, H, S, dv)


def head_group_norm(o, gain, bias):
    B, H, S, dv = o.shape
    of = o.astype(jnp.float32)
    mu = jnp.mean(of, axis=-1, keepdims=True)
    var = jnp.mean(jnp.square(of - mu), axis=-1, keepdims=True)
    on = ((of - mu) * lax.rsqrt(var + NORM_EPS)).astype(o.dtype)
    on = on.transpose(0, 2, 1, 3).reshape(B, S, H * dv)
    return on * gain + bias


def moba_attention(q, k, v):
    B, H, S, d = q.shape
    nb = -(-S // MOBA_BLOCK)
    s_pad = nb * MOBA_BLOCK
    padw = ((0, 0), (0, 0), (0, s_pad - S), (0, 0))
    q, k, v = jnp.pad(q, padw), jnp.pad(k, padw), jnp.pad(v, padw)
    kb = k.reshape(B, H, nb, MOBA_BLOCK, d)
    vb = v.reshape(B, H, nb, MOBA_BLOCK, d)
    k_mean = jnp.mean(kb.astype(jnp.float32), axis=3).astype(k.dtype)
    top = min(MOBA_TOPK, nb)
    scale = MOBA_HEAD_DIM ** -0.5
    n_chunks = s_pad // MOBA_Q_CHUNK
    q_chunks = jnp.moveaxis(q.reshape(B, H, n_chunks, MOBA_Q_CHUNK, d), 2, 0)
    b_ix = jnp.arange(B)[:, None, None, None]
    h_ix = jnp.arange(H)[None, :, None, None]
    blk_ids = jnp.arange(nb)
    slot_ids = jnp.arange(top)

    def one_chunk(args):
        i, q_i = args
        start = i * MOBA_Q_CHUNK
        own = start // MOBA_BLOCK
        t = start + jnp.arange(MOBA_Q_CHUNK)
        gate = jnp.einsum('bhqd,bhnd->bhqn', q_i, k_mean).astype(jnp.float32)
        gate = jnp.where(blk_ids < own, gate, -jnp.inf)
        _, gidx = lax.top_k(gate, top)
        valid = slot_ids < own
        gidx = jnp.where(valid, gidx, 0)
        k_sel = kb[b_ix, h_ix, gidx]
        v_sel = vb[b_ix, h_ix, gidx]
        s_sel = jnp.einsum('bhqd,bhqnkd->bhqnk', q_i, k_sel).astype(jnp.float32) * scale
        s_sel = jnp.where(valid[:, None], s_sel, -jnp.inf).reshape(B, H, MOBA_Q_CHUNK, top * MOBA_BLOCK)
        k_own = lax.dynamic_slice_in_dim(k, own * MOBA_BLOCK, MOBA_BLOCK, axis=2)
        v_own = lax.dynamic_slice_in_dim(v, own * MOBA_BLOCK, MOBA_BLOCK, axis=2)
        kpos = own * MOBA_BLOCK + jnp.arange(MOBA_BLOCK)
        s_own = jnp.einsum('bhqd,bhkd->bhqk', q_i, k_own).astype(jnp.float32) * scale
        s_own = jnp.where(kpos[None, :] <= t[:, None], s_own, -jnp.inf)
        p = jax.nn.softmax(jnp.concatenate([s_sel, s_own], axis=-1), axis=-1).astype(v.dtype)
        p_sel = p[..., :top * MOBA_BLOCK].reshape(B, H, MOBA_Q_CHUNK, top, MOBA_BLOCK)
        p_own = p[..., top * MOBA_BLOCK:]
        return (jnp.einsum('bhqnk,bhqnkd->bhqd', p_sel, v_sel)
                + jnp.einsum('bhqk,bhkd->bhqd', p_own, v_own))

    out = lax.map(one_chunk, (jnp.arange(n_chunks), q_chunks))
    out = jnp.moveaxis(out, 0, 2).reshape(B, H, s_pad, d)
    return out[:, :, :S]


def causal_depthwise_conv(h, w, b):
    S = h.shape[1]
    hp = jnp.pad(h, ((0, 0), (CONV_WIDTH - 1, 0), (0, 0)))
    out = b + w[0] * hp[:, 0:S]
    for i in range(1, CONV_WIDTH):
        out = out + w[i] * hp[:, i:i + S]
    return out


def setup_inputs(seed: int = 0) -> dict:
    key = jax.random.key(seed)
    ks = jax.random.split(key, 18)
    f32 = jnp.float32

    def nrm(k, shape, fan_in):
        return jax.random.normal(k, shape, f32) * (fan_in ** -0.5)

    def gain(k, shape):
        return 1.0 + 0.02 * jax.random.normal(k, shape, f32)

    def small(k, shape):
        return 0.02 * jax.random.normal(k, shape, f32)

    L = DEPTH
    return {
        "x": jax.random.normal(ks[0], (BATCH, SEQ, D_MODEL), f32),
        "attn_norm": gain(ks[1], (L, D_MODEL)),
        "w_in": nrm(ks[2], (L, D_MODEL, D_IN), D_MODEL),
        "ret_norm_gain": gain(ks[3], (L, RET_V)),
        "ret_norm_bias": small(ks[4], (L, RET_V)),
        "moba_q_gain": gain(ks[5], (L, MOBA_HEAD_DIM)),
        "moba_k_gain": gain(ks[6], (L, MOBA_HEAD_DIM)),
        "w_ret_o": nrm(ks[7], (L, RET_V, D_MODEL), RET_V),
        "w_moba_o": nrm(ks[8], (L, MOBA_W, D_MODEL), MOBA_W),
        "w_gate": nrm(ks[9], (L, D_MODEL, 2 * D_MODEL), D_MODEL),
        "b_gate": small(ks[10], (L, 2 * D_MODEL)),
        "w_out": nrm(ks[11], (L, D_MODEL, D_MODEL), D_MODEL),
        "ffn_norm": gain(ks[12], (L, D_MODEL)),
        "w_up": nrm(ks[13], (L, D_MODEL, 2 * D_FF), D_MODEL),
        "conv_w": nrm(ks[14], (L, CONV_WIDTH, 2 * D_FF), CONV_WIDTH),
        "conv_b": small(ks[15], (L, 2 * D_FF)),
        "w_down": nrm(ks[16], (L, D_FF, D_MODEL), D_FF),
    }


def reference(x, attn_norm, w_in, ret_norm_gain, ret_norm_bias, moba_q_gain, moba_k_gain,
              w_ret_o, w_moba_o, w_gate, b_gate, w_out, ffn_norm, w_up, conv_w, conv_b, w_down):
    B, S, _ = x.shape
    pos = jnp.arange(S)
    offsets = tuple(int(o) for o in np.cumsum(IN_SPLITS)[:-1])

    def heads(t, nh):
        return t.reshape(B, S, nh, -1).transpose(0, 2, 1, 3)

    for l in range(DEPTH):
        h = rms_norm(x, attn_norm[l])
        proj = h @ w_in[l]
        q_r, k_r, v_r, g_r, q_m, k_m, v_m = jnp.split(proj, offsets, axis=-1)
        qr = rotary(heads(q_r, RET_HEADS), pos)
        kr = rotary(heads(k_r, RET_HEADS), pos) * (RET_QK_DIM ** -0.5)
        o_r = retention_chunkwise(qr, kr, heads(v_r, RET_HEADS))
        o_r = head_group_norm(o_r, ret_norm_gain[l], ret_norm_bias[l]) * jax.nn.silu(g_r)
        y_ret = o_r @ w_ret_o[l]
        qm = rms_norm(heads(q_m, MOBA_HEADS), moba_q_gain[l])
        km = rms_norm(heads(k_m, MOBA_HEADS), moba_k_gain[l])
        o_m = moba_attention(qm, km, heads(v_m, MOBA_HEADS))
        y_moba = o_m.transpose(0, 2, 1, 3).reshape(B, S, MOBA_W) @ w_moba_o[l]
        gates = jax.nn.sigmoid(h @ w_gate[l] + b_gate[l])
        gate_ret, gate_moba = jnp.split(gates, 2, axis=-1)
        x = x + (gate_ret * y_ret + gate_moba * y_moba) @ w_out[l]
        h = rms_norm(x, ffn_norm[l])
        u = causal_depthwise_conv(h @ w_up[l], conv_w[l], conv_b[l])
        u_gate, u_val = jnp.split(u, 2, axis=-1)
        x = x + (jax.nn.silu(u_gate) * u_val) @ w_down[l]
    return x
```

```python
import functools

import jax
import jax.numpy as jnp
from jax import lax
from jax.experimental import pallas as pl
from jax.experimental.pallas import tpu as pltpu

F32 = jnp.float32
BF16 = jnp.bfloat16

D_MODEL = 1024
RET_HEADS = 4
RET_QK_DIM = 128
RET_V_DIM = 256
ROPE_BASE = 10000.0
MOBA_HEADS = 8
MOBA_HEAD_DIM = 128
MOBA_BLOCK = 256
MOBA_TOPK = 3
D_FF = 2816
CONV_WIDTH = 3
NORM_EPS = 1e-6

RET_QK = RET_HEADS * RET_QK_DIM
RET_V = RET_HEADS * RET_V_DIM
MOBA_W = MOBA_HEADS * MOBA_HEAD_DIM
D_IN = 2 * RET_QK + 2 * RET_V + 3 * MOBA_W
OFF_QR, OFF_KR, OFF_VR, OFF_GR = 0, RET_QK, 2 * RET_QK, 2 * RET_QK + RET_V
OFF_QM = 2 * RET_QK + 2 * RET_V
OFF_KM, OFF_VM = OFF_QM + MOBA_W, OFF_QM + 2 * MOBA_W

V7X_LANES = 128
V7X_SUBLANES = 8
V7X_VMEM_LIMIT_BYTES = 56 * 1024 * 1024

TOKEN_TILE = 512
PROJ_COL_CHUNK = 512
MIX_COL_CHUNK = 512
FF_CHUNK = 256
RET_CHUNK = 256


def _resident(shape):
    nd = len(shape)
    return pl.BlockSpec(shape, lambda *_: (0,) * nd, pipeline_mode=pl.Buffered(1))


def _rms(x, gain):
    inv = lax.rsqrt(jnp.mean(x * x, axis=-1, keepdims=True) + NORM_EPS)
    return (x * inv) * gain


def _inproj_kernel(x_ref, gain_ref, cos_ref, sin_ref, w_ref, o_ref, h_s):
    h_s[...] = _rms(x_ref[...], gain_ref[...]).astype(BF16)
    cos = cos_ref[...]
    sin = sin_ref[...]
    half = RET_QK_DIM // 2
    for c in range(D_IN // PROJ_COL_CHUNK):
        c0 = c * PROJ_COL_CHUNK
        acc = jnp.dot(h_s[...], w_ref[:, c0:c0 + PROJ_COL_CHUNK], preferred_element_type=F32)
        if c0 < OFF_VR:
            k_scale = RET_QK_DIM ** -0.5 if c0 >= OFF_KR else None
            for s in range(PROJ_COL_CHUNK // RET_QK_DIM):
                seg = acc[:, s * RET_QK_DIM:(s + 1) * RET_QK_DIM]
                seg = seg * cos + pltpu.roll(seg, half, axis=1) * sin
                if k_scale is not None:
                    seg = seg * k_scale
                o_ref[:, c0 + s * RET_QK_DIM:c0 + (s + 1) * RET_QK_DIM] = seg.astype(BF16)
        else:
            o_ref[:, c0:c0 + PROJ_COL_CHUNK] = acc.astype(BF16)


def _in_projection(x2, attn_norm, cos, sin, w_in, seq):
    tokens = x2.shape[0]
    tiles_per_seq = seq // TOKEN_TILE
    return pl.pallas_call(
        _inproj_kernel,
        grid=(tokens // TOKEN_TILE,),
        in_specs=[
            pl.BlockSpec((TOKEN_TILE, D_MODEL), lambda i: (i, 0)),
            _resident((1, D_MODEL)),
            pl.BlockSpec((TOKEN_TILE, RET_QK_DIM), lambda i: (i % tiles_per_seq, 0)),
            pl.BlockSpec((TOKEN_TILE, RET_QK_DIM), lambda i: (i % tiles_per_seq, 0)),
            _resident((D_MODEL, D_IN)),
        ],
        out_specs=pl.BlockSpec((TOKEN_TILE, D_IN), lambda i: (i, 0)),
        out_shape=jax.ShapeDtypeStruct((tokens, D_IN), BF16),
        scratch_shapes=[pltpu.VMEM((TOKEN_TILE, D_MODEL), BF16)],
        compiler_params=pltpu.CompilerParams(
            dimension_semantics=("arbitrary",), vmem_limit_bytes=V7X_VMEM_LIMIT_BYTES),
        name="in_projection",
    )(x2, attn_norm, cos, sin, w_in)


def _retention_kernel(q_ref, k_ref, v_ref, g_ref, dec_ref, xi_ref, zeta_ref, gc_ref,
                      gain_ref, bias_ref, o_ref):
    seq = q_ref.shape[0]
    state = jnp.zeros((RET_QK_DIM, RET_V_DIM), F32)
    for c in range(seq // RET_CHUNK):
        rows = slice(c * RET_CHUNK, (c + 1) * RET_CHUNK)
        qc, kc, vc = q_ref[rows, :], k_ref[rows, :], v_ref[rows, :]
        scores = lax.dot_general(qc, kc, (((1,), (1,)), ((), ())),
                                 preferred_element_type=F32) * dec_ref[...]
        o = jnp.dot(scores.astype(BF16), vc, preferred_element_type=F32)
        o = o + jnp.dot(qc, state.astype(BF16), preferred_element_type=F32) * xi_ref[...]
        kz = (kc.astype(F32) * zeta_ref[...]).astype(BF16)
        state = gc_ref[...] * state + lax.dot_general(
            kz, vc, (((0,), (0,)), ((), ())), preferred_element_type=F32)
        mu = jnp.mean(o, axis=-1, keepdims=True)
        d = o - mu
        var = jnp.mean(d * d, axis=-1, keepdims=True)
        on = d * lax.rsqrt(var + NORM_EPS)
        g = g_ref[rows, :].astype(F32)
        o_ref[rows, :] = ((on * gain_ref[...] + bias_ref[...]) * (g * jax.nn.sigmoid(g))).astype(BF16)


def _retention(proj3, decay, xi, zeta, gamma_c, gain, bias):
    batch, seq, _ = proj3.shape
    qk_blk = lambda off: pl.BlockSpec(
        (None, seq, RET_QK_DIM), lambda b, h: (b, 0, off // RET_QK_DIM + h))
    v_blk = lambda off: pl.BlockSpec(
        (None, seq, RET_V_DIM), lambda b, h: (b, 0, off // RET_V_DIM + h))
    per_head = lambda r, c: pl.BlockSpec((None, r, c), lambda b, h: (h, 0, 0))
    return pl.pallas_call(
        _retention_kernel,
        grid=(batch, RET_HEADS),
        in_specs=[
            qk_blk(OFF_QR), qk_blk(OFF_KR), v_blk(OFF_VR), v_blk(OFF_GR),
            per_head(RET_CHUNK, RET_CHUNK), per_head(RET_CHUNK, RET_V_DIM),
            per_head(RET_CHUNK, RET_QK_DIM), per_head(1, RET_V_DIM),
            pl.BlockSpec((1, RET_V_DIM), lambda b, h: (0, h)),
            pl.BlockSpec((1, RET_V_DIM), lambda b, h: (0, h)),
        ],
        out_specs=pl.BlockSpec((None, seq, RET_V_DIM), lambda b, h: (b, 0, h)),
        out_shape=jax.ShapeDtypeStruct((batch, seq, RET_V), BF16),
        compiler_params=pltpu.CompilerParams(
            dimension_semantics=("arbitrary", "arbitrary"), vmem_limit_bytes=V7X_VMEM_LIMIT_BYTES),
        name="retention",
    )(proj3, proj3, proj3, proj3, decay, xi, zeta, gamma_c, gain, bias)


def _moba_kernel(q_ref, k_ref, v_ref, gq_ref, gk_ref, o_ref, qn_s, kn_s, vt_s, p_s):
    seq = q_ref.shape[0]
    nb = seq // MOBA_BLOCK
    scale = MOBA_HEAD_DIM ** -0.5
    qn = _rms(q_ref[...].astype(F32), gq_ref[...])
    kn = _rms(k_ref[...].astype(F32), gk_ref[...])
    k_mean = jnp.mean(kn.reshape(nb, MOBA_BLOCK, MOBA_HEAD_DIM), axis=1)
    gate_t = lax.dot_general(k_mean, qn, (((1,), (1,)), ((), ())),
                             precision=lax.Precision.HIGHEST,
                             preferred_element_type=F32)
    qn_s[...] = qn.astype(BF16)
    kn_s[...] = kn.astype(BF16)
    vt_s[...] = v_ref[...].astype(F32).T.astype(BF16)

    key_i = lax.broadcasted_iota(jnp.int32, (MOBA_BLOCK, MOBA_BLOCK), 0)
    qry_i = lax.broadcasted_iota(jnp.int32, (MOBA_BLOCK, MOBA_BLOCK), 1)
    causal = key_i <= qry_i

    for qb in range(nb):
        cols = slice(qb * MOBA_BLOCK, (qb + 1) * MOBA_BLOCK)
        nk = (qb + 1) * MOBA_BLOCK
        s_t = lax.dot_general(kn_s[0:nk, :], qn_s[cols, :], (((1,), (1,)), ((), ())),
                              preferred_element_type=F32) * scale
        keep = [None] * qb
        if qb > MOBA_TOPK:
            g = [gate_t[i:i + 1, cols] for i in range(qb)]
            for j in range(qb):
                beaten = jnp.zeros((1, MOBA_BLOCK), jnp.int32)
                for i in range(qb):
                    if i < j:
                        beaten = beaten + (g[i] >= g[j]).astype(jnp.int32)
                    elif i > j:
                        beaten = beaten + (g[i] > g[j]).astype(jnp.int32)
                keep[j] = beaten < MOBA_TOPK
        pieces = []
        for j in range(qb + 1):
            piece = s_t[j * MOBA_BLOCK:(j + 1) * MOBA_BLOCK, :]
            if j == qb:
                piece = jnp.where(causal, piece, -jnp.inf)
            elif keep[j] is not None:
                piece = jnp.where(keep[j], piece, -jnp.inf)
            pieces.append(piece)
        m = pieces[0].max(axis=0, keepdims=True)
        for piece in pieces[1:]:
            m = jnp.maximum(m, piece.max(axis=0, keepdims=True))
        denom = jnp.zeros((1, MOBA_BLOCK), F32)
        for j, piece in enumerate(pieces):
            p = jnp.exp(piece - m)
            denom = denom + p.sum(axis=0, keepdims=True)
            p_s[j * MOBA_BLOCK:(j + 1) * MOBA_BLOCK, :] = p.astype(BF16)
        o_t = jnp.dot(vt_s[:, 0:nk], p_s[0:nk, :], preferred_element_type=F32)
        o_ref[cols, :] = (o_t / denom).T.astype(BF16)


def _moba(proj3, q_gain, k_gain):
    batch, seq, _ = proj3.shape
    head_blk = lambda off: pl.BlockSpec(
        (None, seq, MOBA_HEAD_DIM), lambda b, h: (b, 0, off // MOBA_HEAD_DIM + h))
    return pl.pallas_call(
        _moba_kernel,
        grid=(batch, MOBA_HEADS),
        in_specs=[
            head_blk(OFF_QM), head_blk(OFF_KM), head_blk(OFF_VM),
            pl.BlockSpec((1, MOBA_HEAD_DIM), lambda b, h: (0, 0)),
            pl.BlockSpec((1, MOBA_HEAD_DIM), lambda b, h: (0, 0)),
        ],
        out_specs=pl.BlockSpec((None, seq, MOBA_HEAD_DIM), lambda b, h: (b, 0, h)),
        out_shape=jax.ShapeDtypeStruct((batch, seq, MOBA_W), BF16),
        scratch_shapes=[
            pltpu.VMEM((seq, MOBA_HEAD_DIM), BF16),
            pltpu.VMEM((seq, MOBA_HEAD_DIM), BF16),
            pltpu.VMEM((MOBA_HEAD_DIM, seq), BF16),
            pltpu.VMEM((seq, MOBA_BLOCK), BF16),
        ],
        compiler_params=pltpu.CompilerParams(
            dimension_semantics=("arbitrary", "arbitrary"), vmem_limit_bytes=V7X_VMEM_LIMIT_BYTES),
        name="block_attention",
    )(proj3, proj3, proj3, q_gain, k_gain)


def _mix_ffn_kernel(x_ref, or_ref, om_ref, an_ref, wg_ref, bg_ref, wro_ref, wmo_ref, wo_ref,
                    fn_ref, wup_ref, cw_ref, cb_ref, wdn_ref, out_ref,
                    h_s, merged_s, act_s, ubuf_s, carry_s, *, tiles_per_seq):
    tm = x_ref.shape[0]
    halo = V7X_SUBLANES

    @pl.when(pl.program_id(0) % tiles_per_seq == 0)
    def _():
        carry_s[...] = jnp.zeros_like(carry_s)

    x = x_ref[...]
    h_s[...] = _rms(x, an_ref[...]).astype(BF16)
    for c in range(D_MODEL // MIX_COL_CHUNK):
        cs = slice(c * MIX_COL_CHUNK, (c + 1) * MIX_COL_CHUNK)
        cs2 = slice(D_MODEL + c * MIX_COL_CHUNK, D_MODEL + (c + 1) * MIX_COL_CHUNK)
        y_ret = jnp.dot(or_ref[...], wro_ref[:, cs], preferred_element_type=F32)
        y_moba = jnp.dot(om_ref[...], wmo_ref[:, cs], preferred_element_type=F32)
        g_ret = jax.nn.sigmoid(
            jnp.dot(h_s[...], wg_ref[:, cs], preferred_element_type=F32) + bg_ref[:, cs])
        g_moba = jax.nn.sigmoid(
            jnp.dot(h_s[...], wg_ref[:, cs2], preferred_element_type=F32) + bg_ref[:, cs2])
        merged_s[:, cs] = (g_ret * y_ret + g_moba * y_moba).astype(BF16)
    x1 = x + jnp.dot(merged_s[...], wo_ref[...], preferred_element_type=F32)
    out_ref[...] = x1
    h_s[...] = _rms(x1, fn_ref[...]).astype(BF16)

    def conv(u, slot, cols):
        ubuf_s[slot, 0:halo, :] = carry_s[:, cols]
        ubuf_s[slot, halo:halo + tm, :] = u
        carry_s[:, cols] = u[tm - halo:tm, :]
        u1 = ubuf_s[slot, halo - 1:halo - 1 + tm, :]
        u2 = ubuf_s[slot, halo - 2:halo - 2 + tm, :]
        return (cb_ref[:, cols] + cw_ref[0:1, cols] * u2 + cw_ref[1:2, cols] * u1
                + cw_ref[2:3, cols] * u)

    for c in range(D_FF // FF_CHUNK):
        gcols = slice(c * FF_CHUNK, (c + 1) * FF_CHUNK)
        vcols = slice(D_FF + c * FF_CHUNK, D_FF + (c + 1) * FF_CHUNK)
        u_gate = conv(jnp.dot(h_s[...], wup_ref[:, gcols], preferred_element_type=F32), 0, gcols)
        u_val = conv(jnp.dot(h_s[...], wup_ref[:, vcols], preferred_element_type=F32), 1, vcols)
        act_s[:, gcols] = (u_gate * jax.nn.sigmoid(u_gate) * u_val).astype(BF16)
    out_ref[...] += jnp.dot(act_s[...], wdn_ref[...], preferred_element_type=F32)


def _mix_ffn(x2, o_r, o_m, attn_norm, w_gate, b_gate, w_ret_o, w_moba_o, w_out,
             ffn_norm, w_up, conv_w, conv_b, w_down, seq):
    tokens = x2.shape[0]
    tile = lambda cols: pl.BlockSpec((TOKEN_TILE, cols), lambda i: (i, 0))
    return pl.pallas_call(
        functools.partial(_mix_ffn_kernel, tiles_per_seq=seq // TOKEN_TILE),
        grid=(tokens // TOKEN_TILE,),
        in_specs=[
            tile(D_MODEL), tile(RET_V), tile(MOBA_W),
            _resident((1, D_MODEL)), _resident((D_MODEL, 2 * D_MODEL)), _resident((1, 2 * D_MODEL)),
            _resident((RET_V, D_MODEL)), _resident((MOBA_W, D_MODEL)), _resident((D_MODEL, D_MODEL)),
            _resident((1, D_MODEL)), _resident((D_MODEL, 2 * D_FF)),
            _resident((CONV_WIDTH, 2 * D_FF)), _resident((1, 2 * D_FF)), _resident((D_FF, D_MODEL)),
        ],
        out_specs=tile(D_MODEL),
        out_shape=jax.ShapeDtypeStruct((tokens, D_MODEL), F32),
        scratch_shapes=[
            pltpu.VMEM((TOKEN_TILE, D_MODEL), BF16),
            pltpu.VMEM((TOKEN_TILE, D_MODEL), BF16),
            pltpu.VMEM((TOKEN_TILE, D_FF), BF16),
            pltpu.VMEM((2, TOKEN_TILE + V7X_SUBLANES, FF_CHUNK), F32),
            pltpu.VMEM((V7X_SUBLANES, 2 * D_FF), F32),
        ],
        compiler_params=pltpu.CompilerParams(
            dimension_semantics=("arbitrary",), vmem_limit_bytes=V7X_VMEM_LIMIT_BYTES),
        name="merge_ffn",
    )(x2, o_r, o_m, attn_norm, w_gate, b_gate, w_ret_o, w_moba_o, w_out,
      ffn_norm, w_up, conv_w, conv_b, w_down)


def _rotary_tables(seq):
    half = RET_QK_DIM // 2
    inv_freq = ROPE_BASE ** (-jnp.arange(half, dtype=F32) / half)
    ang = jnp.arange(seq).astype(F32)[:, None] * inv_freq[None, :]
    cos, sin = jnp.cos(ang), jnp.sin(ang)
    return jnp.concatenate([cos, cos], axis=-1), jnp.concatenate([-sin, sin], axis=-1)


def _retention_tables():
    log_gamma = jnp.log1p(-jnp.power(2.0, -5.0 - jnp.arange(RET_HEADS, dtype=F32)))
    pos = jnp.arange(RET_CHUNK, dtype=F32)
    diff = pos[:, None] - pos[None, :]
    decay = jnp.where(diff >= 0, jnp.exp(log_gamma[:, None, None] * jnp.maximum(diff, 0.0)), 0.0)
    zeta = jnp.exp(log_gamma[:, None] * (RET_CHUNK - 1 - pos)[None, :])
    xi = jnp.exp(log_gamma[:, None] * (pos + 1)[None, :])
    gamma_c = jnp.exp(log_gamma * RET_CHUNK)
    xi_b = jnp.broadcast_to(xi[:, :, None], (RET_HEADS, RET_CHUNK, RET_V_DIM))
    zeta_b = jnp.broadcast_to(zeta[:, :, None], (RET_HEADS, RET_CHUNK, RET_QK_DIM))
    gamma_b = jnp.broadcast_to(gamma_c[:, None, None], (RET_HEADS, 1, RET_V_DIM))
    return decay, xi_b, zeta_b, gamma_b


def kernel(x, attn_norm, w_in, ret_norm_gain, ret_norm_bias, moba_q_gain, moba_k_gain,
           w_ret_o, w_moba_o, w_gate, b_gate, w_out, ffn_norm, w_up, conv_w, conv_b, w_down):
    batch, seq, d_model = x.shape
    depth = attn_norm.shape[0]
    assert d_model == D_MODEL and seq % TOKEN_TILE == 0 and seq % MOBA_BLOCK == 0
    cos, sin = _rotary_tables(seq)
    decay, xi_b, zeta_b, gamma_b = _retention_tables()
    x2 = x.reshape(batch * seq, d_model)
    for l in range(depth):
        proj = _in_projection(x2, attn_norm[l][None], cos, sin, w_in[l].astype(BF16), seq)
        proj3 = proj.reshape(batch, seq, D_IN)
        o_r = _retention(proj3, decay, xi_b, zeta_b, gamma_b,
                         ret_norm_gain[l][None], ret_norm_bias[l][None])
        o_m = _moba(proj3, moba_q_gain[l][None], moba_k_gain[l][None])
        x2 = _mix_ffn(
            x2, o_r.reshape(batch * seq, RET_V), o_m.reshape(batch * seq, MOBA_W),
            attn_norm[l][None], w_gate[l].astype(BF16), b_gate[l][None],
            w_ret_o[l].astype(BF16), w_moba_o[l].astype(BF16), w_out[l].astype(BF16),
            ffn_norm[l][None], w_up[l].astype(BF16), conv_w[l], conv_b[l][None],
            w_down[l].astype(BF16), seq)
    return x2.reshape(batch, seq, d_model)
```

```python
import functools

import jax
import jax.numpy as jnp
from jax import lax
from jax.experimental import pallas as pl
from jax.experimental.pallas import tpu as pltpu

F32 = jnp.float32
BF16 = jnp.bfloat16

D_MODEL = 1024
RET_HEADS = 4
RET_QK_DIM = 128
RET_V_DIM = 256
ROPE_BASE = 10000.0
MOBA_HEADS = 8
MOBA_HEAD_DIM = 128
MOBA_BLOCK = 256
MOBA_TOPK = 3
D_FF = 2816
CONV_WIDTH = 3
NORM_EPS = 1e-6

RET_QK = RET_HEADS * RET_QK_DIM
RET_V = RET_HEADS * RET_V_DIM
MOBA_W = MOBA_HEADS * MOBA_HEAD_DIM
D_IN = 2 * RET_QK + 2 * RET_V + 3 * MOBA_W
OFF_QR, OFF_KR, OFF_VR, OFF_GR = 0, RET_QK, 2 * RET_QK, 2 * RET_QK + RET_V
OFF_QM = 2 * RET_QK + 2 * RET_V
OFF_KM, OFF_VM = OFF_QM + MOBA_W, OFF_QM + 2 * MOBA_W

V7X_LANES = 128
V7X_SUBLANES = 8
V7X_VMEM_LIMIT_BYTES = 56 * 1024 * 1024

TOKEN_TILE = 512
PROJ_COL_CHUNK = 512
MIX_COL_CHUNK = 512
FF_CHUNK = 256
RET_CHUNK = 256


def _resident(shape):
    nd = len(shape)
    return pl.BlockSpec(shape, lambda *_: (0,) * nd, pipeline_mode=pl.Buffered(1))


def _rms(x, gain):
    inv = lax.rsqrt(jnp.mean(x * x, axis=-1, keepdims=True) + NORM_EPS)
    return (x * inv) * gain


def _inproj_kernel(x_ref, gain_ref, cos_ref, sin_ref, w_ref, o_ref, h_s):
    h_s[...] = _rms(x_ref[...], gain_ref[...]).astype(BF16)
    cos = cos_ref[...]
    sin = sin_ref[...]
    half = RET_QK_DIM // 2
    for c in range(D_IN // PROJ_COL_CHUNK):
        c0 = c * PROJ_COL_CHUNK
        acc = jnp.dot(h_s[...], w_ref[:, c0:c0 + PROJ_COL_CHUNK], preferred_element_type=F32)
        if c0 < OFF_VR:
            k_scale = RET_QK_DIM ** -0.5 if c0 >= OFF_KR else None
            for s in range(PROJ_COL_CHUNK // RET_QK_DIM):
                seg = acc[:, s * RET_QK_DIM:(s + 1) * RET_QK_DIM]
                seg = seg * cos + pltpu.roll(seg, half, axis=1) * sin
                if k_scale is not None:
                    seg = seg * k_scale
                o_ref[:, c0 + s * RET_QK_DIM:c0 + (s + 1) * RET_QK_DIM] = seg.astype(BF16)
        else:
            o_ref[:, c0:c0 + PROJ_COL_CHUNK] = acc.astype(BF16)


def _in_projection(x2, attn_norm, cos, sin, w_in, seq):
    tokens = x2.shape[0]
    tiles_per_seq = seq // TOKEN_TILE
    return pl.pallas_call(
        _inproj_kernel,
        grid=(tokens // TOKEN_TILE,),
        in_specs=[
            pl.BlockSpec((TOKEN_TILE, D_MODEL), lambda i: (i, 0)),
            _resident((1, D_MODEL)),
            pl.BlockSpec((TOKEN_TILE, RET_QK_DIM), lambda i: (i % tiles_per_seq, 0)),
            pl.BlockSpec((TOKEN_TILE, RET_QK_DIM), lambda i: (i % tiles_per_seq, 0)),
            _resident((D_MODEL, D_IN)),
        ],
        out_specs=pl.BlockSpec((TOKEN_TILE, D_IN), lambda i: (i, 0)),
        out_shape=jax.ShapeDtypeStruct((tokens, D_IN), BF16),
        scratch_shapes=[pltpu.VMEM((TOKEN_TILE, D_MODEL), BF16)],
        compiler_params=pltpu.CompilerParams(
            dimension_semantics=("arbitrary",), vmem_limit_bytes=V7X_VMEM_LIMIT_BYTES),
        name="in_projection",
    )(x2, attn_norm, cos, sin, w_in)


def _retention_kernel(q_ref, k_ref, v_ref, g_ref, dec_ref, xi_ref, zeta_ref, gc_ref,
                      gain_ref, bias_ref, o_ref):
    seq = q_ref.shape[0]
    state = jnp.zeros((RET_QK_DIM, RET_V_DIM), F32)
    for c in range(seq // RET_CHUNK):
        rows = slice(c * RET_CHUNK, (c + 1) * RET_CHUNK)
        qc, kc, vc = q_ref[rows, :], k_ref[rows, :], v_ref[rows, :]
        scores = lax.dot_general(qc, kc, (((1,), (1,)), ((), ())),
                                 preferred_element_type=F32) * dec_ref[...]
        o = jnp.dot(scores.astype(BF16), vc, preferred_element_type=F32)
        o = o + jnp.dot(qc, state.astype(BF16), preferred_element_type=F32) * xi_ref[...]
        kz = (kc.astype(F32) * zeta_ref[...]).astype(BF16)
        state = gc_ref[...] * state + lax.dot_general(
            kz, vc, (((0,), (0,)), ((), ())), preferred_element_type=F32)
        mu = jnp.mean(o, axis=-1, keepdims=True)
        d = o - mu
        var = jnp.mean(d * d, axis=-1, keepdims=True)
        on = d * lax.rsqrt(var + NORM_EPS)
        g = g_ref[rows, :].astype(F32)
        o_ref[rows, :] = ((on * gain_ref[...] + bias_ref[...]) * (g * jax.nn.sigmoid(g))).astype(BF16)


def _retention(proj3, decay, xi, zeta, gamma_c, gain, bias):
    batch, seq, _ = proj3.shape
    qk_blk = lambda off: pl.BlockSpec(
        (None, seq, RET_QK_DIM), lambda b, h: (b, 0, off // RET_QK_DIM + h))
    v_blk = lambda off: pl.BlockSpec(
        (None, seq, RET_V_DIM), lambda b, h: (b, 0, off // RET_V_DIM + h))
    per_head = lambda r, c: pl.BlockSpec((None, r, c), lambda b, h: (h, 0, 0))
    return pl.pallas_call(
        _retention_kernel,
        grid=(batch, RET_HEADS),
        in_specs=[
            qk_blk(OFF_QR), qk_blk(OFF_KR), v_blk(OFF_VR), v_blk(OFF_GR),
            per_head(RET_CHUNK, RET_CHUNK), per_head(RET_CHUNK, RET_V_DIM),
            per_head(RET_CHUNK, RET_QK_DIM), per_head(1, RET_V_DIM),
            pl.BlockSpec((1, RET_V_DIM), lambda b, h: (0, h)),
            pl.BlockSpec((1, RET_V_DIM), lambda b, h: (0, h)),
        ],
        out_specs=pl.BlockSpec((None, seq, RET_V_DIM), lambda b, h: (b, 0, h)),
        out_shape=jax.ShapeDtypeStruct((batch, seq, RET_V), BF16),
        compiler_params=pltpu.CompilerParams(
            dimension_semantics=("arbitrary", "arbitrary"), vmem_limit_bytes=V7X_VMEM_LIMIT_BYTES),
        name="retention",
    )(proj3, proj3, proj3, proj3, decay, xi, zeta, gamma_c, gain, bias)


def _moba_kernel(q_ref, k_ref, v_ref, gq_ref, gk_ref, o_ref, qn_s, kn_s, vt_s, s_s, p_s):
    seq = q_ref.shape[0]
    nb = seq // MOBA_BLOCK
    exp2_scale = MOBA_HEAD_DIM ** -0.5 * 1.4426950408889634
    qn = _rms(q_ref[...].astype(F32), gq_ref[...])
    kn = _rms(k_ref[...].astype(F32), gk_ref[...])
    k_mean = jnp.mean(kn.reshape(nb, MOBA_BLOCK, MOBA_HEAD_DIM), axis=1)
    gate_t = lax.dot_general(k_mean, qn, (((1,), (1,)), ((), ())),
                             precision=lax.Precision.HIGHEST,
                             preferred_element_type=F32)
    qn_s[...] = qn.astype(BF16)
    kn_s[...] = kn.astype(BF16)
    vt_s[...] = v_ref[...].astype(F32).T.astype(BF16)

    key_i = lax.broadcasted_iota(jnp.int32, (MOBA_BLOCK, MOBA_BLOCK), 0)
    qry_i = lax.broadcasted_iota(jnp.int32, (MOBA_BLOCK, MOBA_BLOCK), 1)
    causal = key_i <= qry_i

    def blk(j):
        return slice(j * MOBA_BLOCK, (j + 1) * MOBA_BLOCK)

    def keep_masks(qb):
        if qb <= MOBA_TOPK:
            return [None] * qb
        g = [gate_t[i:i + 1, blk(qb)] for i in range(qb)]
        keep = []
        for j in range(qb):
            beaten = jnp.zeros((1, MOBA_BLOCK), jnp.int32)
            for i in range(qb):
                if i < j:
                    beaten = beaten + (g[i] >= g[j]).astype(jnp.int32)
                elif i > j:
                    beaten = beaten + (g[i] > g[j]).astype(jnp.int32)
            keep.append(beaten < MOBA_TOPK)
        return keep

    def scores_stage(qb):
        slot = qb % 2
        keep = keep_masks(qb)
        q_blk = qn_s[blk(qb), :]
        m = None
        for j in range(qb + 1):
            s = lax.dot_general(kn_s[blk(j), :], q_blk, (((1,), (1,)), ((), ())),
                                preferred_element_type=F32)
            if j == qb:
                s = jnp.where(causal, s, -jnp.inf)
            elif keep[j] is not None:
                s = jnp.where(keep[j], s, -jnp.inf)
            s_s[slot, blk(j), :] = s
            mj = s.max(axis=0, keepdims=True)
            m = mj if m is None else jnp.maximum(m, mj)
        return m

    def softmax_stage(qb, m):
        slot = qb % 2
        denom = jnp.zeros((1, MOBA_BLOCK), F32)
        for j in range(qb + 1):
            p = jnp.exp2((s_s[slot, blk(j), :] - m) * exp2_scale)
            denom = denom + p.sum(axis=0, keepdims=True)
            p_s[slot, blk(j), :] = p.astype(BF16)
        return denom

    def value_stage(qb, denom):
        slot = qb % 2
        nk = (qb + 1) * MOBA_BLOCK
        o_t = jnp.dot(vt_s[:, 0:nk], p_s[slot, 0:nk, :], preferred_element_type=F32)
        o_ref[blk(qb), :] = (o_t / denom).T.astype(BF16)

    m_next = scores_stage(0)
    for qb in range(nb):
        m_cur = m_next
        if qb + 1 < nb:
            m_next = scores_stage(qb + 1)
        value_stage(qb, softmax_stage(qb, m_cur))


def _moba(proj3, q_gain, k_gain):
    batch, seq, _ = proj3.shape
    head_blk = lambda off: pl.BlockSpec(
        (None, seq, MOBA_HEAD_DIM), lambda b, h: (b, 0, off // MOBA_HEAD_DIM + h))
    return pl.pallas_call(
        _moba_kernel,
        grid=(batch, MOBA_HEADS),
        in_specs=[
            head_blk(OFF_QM), head_blk(OFF_KM), head_blk(OFF_VM),
            pl.BlockSpec((1, MOBA_HEAD_DIM), lambda b, h: (0, 0)),
            pl.BlockSpec((1, MOBA_HEAD_DIM), lambda b, h: (0, 0)),
        ],
        out_specs=pl.BlockSpec((None, seq, MOBA_HEAD_DIM), lambda b, h: (b, 0, h)),
        out_shape=jax.ShapeDtypeStruct((batch, seq, MOBA_W), BF16),
        scratch_shapes=[
            pltpu.VMEM((seq, MOBA_HEAD_DIM), BF16),
            pltpu.VMEM((seq, MOBA_HEAD_DIM), BF16),
            pltpu.VMEM((MOBA_HEAD_DIM, seq), BF16),
            pltpu.VMEM((2, seq, MOBA_BLOCK), F32),
            pltpu.VMEM((2, seq, MOBA_BLOCK), BF16),
        ],
        compiler_params=pltpu.CompilerParams(
            dimension_semantics=("arbitrary", "arbitrary"), vmem_limit_bytes=V7X_VMEM_LIMIT_BYTES),
        name="block_attention",
    )(proj3, proj3, proj3, q_gain, k_gain)


def _mix_ffn_kernel(x_ref, or_ref, om_ref, an_ref, wg_ref, bg_ref, wro_ref, wmo_ref, wo_ref,
                    fn_ref, wup_ref, cw_ref, cb_ref, wdn_ref, out_ref,
                    h_s, merged_s, act_s, ubuf_s, carry_s, *, tiles_per_seq):
    tm = x_ref.shape[0]
    halo = V7X_SUBLANES

    @pl.when(pl.program_id(0) % tiles_per_seq == 0)
    def _():
        carry_s[...] = jnp.zeros_like(carry_s)

    x = x_ref[...]
    h_s[...] = _rms(x, an_ref[...]).astype(BF16)
    for c in range(D_MODEL // MIX_COL_CHUNK):
        cs = slice(c * MIX_COL_CHUNK, (c + 1) * MIX_COL_CHUNK)
        cs2 = slice(D_MODEL + c * MIX_COL_CHUNK, D_MODEL + (c + 1) * MIX_COL_CHUNK)
        y_ret = jnp.dot(or_ref[...], wro_ref[:, cs], preferred_element_type=F32)
        y_moba = jnp.dot(om_ref[...], wmo_ref[:, cs], preferred_element_type=F32)
        g_ret = jax.nn.sigmoid(
            jnp.dot(h_s[...], wg_ref[:, cs], preferred_element_type=F32) + bg_ref[:, cs])
        g_moba = jax.nn.sigmoid(
            jnp.dot(h_s[...], wg_ref[:, cs2], preferred_element_type=F32) + bg_ref[:, cs2])
        merged_s[:, cs] = (g_ret * y_ret + g_moba * y_moba).astype(BF16)
    x1 = x + jnp.dot(merged_s[...], wo_ref[...], preferred_element_type=F32)
    out_ref[...] = x1
    h_s[...] = _rms(x1, fn_ref[...]).astype(BF16)

    def conv(u, slot, cols):
        ubuf_s[slot, 0:halo, :] = carry_s[:, cols]
        ubuf_s[slot, halo:halo + tm, :] = u
        carry_s[:, cols] = u[tm - halo:tm, :]
        u1 = ubuf_s[slot, halo - 1:halo - 1 + tm, :]
        u2 = ubuf_s[slot, halo - 2:halo - 2 + tm, :]
        return (cb_ref[:, cols] + cw_ref[0:1, cols] * u2 + cw_ref[1:2, cols] * u1
                + cw_ref[2:3, cols] * u)

    for c in range(D_FF // FF_CHUNK):
        gcols = slice(c * FF_CHUNK, (c + 1) * FF_CHUNK)
        vcols = slice(D_FF + c * FF_CHUNK, D_FF + (c + 1) * FF_CHUNK)
        u_gate = conv(jnp.dot(h_s[...], wup_ref[:, gcols], preferred_element_type=F32), 0, gcols)
        u_val = conv(jnp.dot(h_s[...], wup_ref[:, vcols], preferred_element_type=F32), 1, vcols)
        act_s[:, gcols] = (u_gate * jax.nn.sigmoid(u_gate) * u_val).astype(BF16)
    out_ref[...] += jnp.dot(act_s[...], wdn_ref[...], preferred_element_type=F32)


def _mix_ffn(x2, o_r, o_m, attn_norm, w_gate, b_gate, w_ret_o, w_moba_o, w_out,
             ffn_norm, w_up, conv_w, conv_b, w_down, seq):
    tokens = x2.shape[0]
    tile = lambda cols: pl.BlockSpec((TOKEN_TILE, cols), lambda i: (i, 0))
    return pl.pallas_call(
        functools.partial(_mix_ffn_kernel, tiles_per_seq=seq // TOKEN_TILE),
        grid=(tokens // TOKEN_TILE,),
        in_specs=[
            tile(D_MODEL), tile(RET_V), tile(MOBA_W),
            _resident((1, D_MODEL)), _resident((D_MODEL, 2 * D_MODEL)), _resident((1, 2 * D_MODEL)),
            _resident((RET_V, D_MODEL)), _resident((MOBA_W, D_MODEL)), _resident((D_MODEL, D_MODEL)),
            _resident((1, D_MODEL)), _resident((D_MODEL, 2 * D_FF)),
            _resident((CONV_WIDTH, 2 * D_FF)), _resident((1, 2 * D_FF)), _resident((D_FF, D_MODEL)),
        ],
        out_specs=tile(D_MODEL),
        out_shape=jax.ShapeDtypeStruct((tokens, D_MODEL), F32),
        scratch_shapes=[
            pltpu.VMEM((TOKEN_TILE, D_MODEL), BF16),
            pltpu.VMEM((TOKEN_TILE, D_MODEL), BF16),
            pltpu.VMEM((TOKEN_TILE, D_FF), BF16),
            pltpu.VMEM((2, TOKEN_TILE + V7X_SUBLANES, FF_CHUNK), F32),
            pltpu.VMEM((V7X_SUBLANES, 2 * D_FF), F32),
        ],
        compiler_params=pltpu.CompilerParams(
            dimension_semantics=("arbitrary",), vmem_limit_bytes=V7X_VMEM_LIMIT_BYTES),
        name="merge_ffn",
    )(x2, o_r, o_m, attn_norm, w_gate, b_gate, w_ret_o, w_moba_o, w_out,
      ffn_norm, w_up, conv_w, conv_b, w_down)


def _rotary_tables(seq):
    half = RET_QK_DIM // 2
    inv_freq = ROPE_BASE ** (-jnp.arange(half, dtype=F32) / half)
    ang = jnp.arange(seq).astype(F32)[:, None] * inv_freq[None, :]
    cos, sin = jnp.cos(ang), jnp.sin(ang)
    return jnp.concatenate([cos, cos], axis=-1), jnp.concatenate([-sin, sin], axis=-1)


def _retention_tables():
    log_gamma = jnp.log1p(-jnp.power(2.0, -5.0 - jnp.arange(RET_HEADS, dtype=F32)))
    pos = jnp.arange(RET_CHUNK, dtype=F32)
    diff = pos[:, None] - pos[None, :]
    decay = jnp.where(diff >= 0, jnp.exp(log_gamma[:, None, None] * jnp.maximum(diff, 0.0)), 0.0)
    zeta = jnp.exp(log_gamma[:, None] * (RET_CHUNK - 1 - pos)[None, :])
    xi = jnp.exp(log_gamma[:, None] * (pos + 1)[None, :])
    gamma_c = jnp.exp(log_gamma * RET_CHUNK)
    xi_b = jnp.broadcast_to(xi[:, :, None], (RET_HEADS, RET_CHUNK, RET_V_DIM))
    zeta_b = jnp.broadcast_to(zeta[:, :, None], (RET_HEADS, RET_CHUNK, RET_QK_DIM))
    gamma_b = jnp.broadcast_to(gamma_c[:, None, None], (RET_HEADS, 1, RET_V_DIM))
    return decay, xi_b, zeta_b, gamma_b


def kernel(x, attn_norm, w_in, ret_norm_gain, ret_norm_bias, moba_q_gain, moba_k_gain,
           w_ret_o, w_moba_o, w_gate, b_gate, w_out, ffn_norm, w_up, conv_w, conv_b, w_down):
    batch, seq, d_model = x.shape
    depth = attn_norm.shape[0]
    assert d_model == D_MODEL and seq % TOKEN_TILE == 0 and seq % MOBA_BLOCK == 0
    cos, sin = _rotary_tables(seq)
    decay, xi_b, zeta_b, gamma_b = _retention_tables()
    x2 = x.reshape(batch * seq, d_model)
    for l in range(depth):
        proj = _in_projection(x2, attn_norm[l][None], cos, sin, w_in[l].astype(BF16), seq)
        proj3 = proj.reshape(batch, seq, D_IN)
        o_r = _retention(proj3, decay, xi_b, zeta_b, gamma_b,
                         ret_norm_gain[l][None], ret_norm_bias[l][None])
        o_m = _moba(proj3, moba_q_gain[l][None], moba_k_gain[l][None])
        x2 = _mix_ffn(
            x2, o_r.reshape(batch * seq, RET_V), o_m.reshape(batch * seq, MOBA_W),
            attn_norm[l][None], w_gate[l].astype(BF16), b_gate[l][None],
            w_ret_o[l].astype(BF16), w_moba_o[l].astype(BF16), w_out[l].astype(BF16),
            ffn_norm[l][None], w_up[l].astype(BF16), conv_w[l], conv_b[l][None],
            w_down[l].astype(BF16), seq)
    return x2.reshape(batch, seq, d_model)
```

```python
import functools

import jax
import jax.numpy as jnp
from jax import lax
from jax.experimental import pallas as pl
from jax.experimental.pallas import tpu as pltpu

F32 = jnp.float32
BF16 = jnp.bfloat16

D_MODEL = 1024
RET_HEADS = 4
RET_QK_DIM = 128
RET_V_DIM = 256
ROPE_BASE = 10000.0
MOBA_HEADS = 8
MOBA_HEAD_DIM = 128
MOBA_BLOCK = 256
MOBA_TOPK = 3
D_FF = 2816
CONV_WIDTH = 3
NORM_EPS = 1e-6
LOG2_E = 1.4426950408889634

RET_QK = RET_HEADS * RET_QK_DIM
RET_V = RET_HEADS * RET_V_DIM
MOBA_W = MOBA_HEADS * MOBA_HEAD_DIM
D_IN = 2 * RET_QK + 2 * RET_V + 3 * MOBA_W
OFF_QR, OFF_KR, OFF_VR, OFF_GR = 0, RET_QK, 2 * RET_QK, 2 * RET_QK + RET_V
OFF_QM = 2 * RET_QK + 2 * RET_V
OFF_KM, OFF_VM = OFF_QM + MOBA_W, OFF_QM + 2 * MOBA_W

V7X_LANES = 128
V7X_SUBLANES = 8
V7X_VMEM_LIMIT_BYTES = 56 * 1024 * 1024

TOKEN_TILE = 512
PROJ_COL_CHUNK = 512
MIX_COL_CHUNK = 512
FF_CHUNK = 256
RET_CHUNK = 256


def _resident(shape):
    nd = len(shape)
    return pl.BlockSpec(shape, lambda *_: (0,) * nd, pipeline_mode=pl.Buffered(1))


def _rms(x, gain):
    inv = lax.rsqrt(jnp.mean(x * x, axis=-1, keepdims=True) + NORM_EPS)
    return (x * inv) * gain


def _inproj_kernel(x_ref, gain_ref, cos_ref, sin_ref, gq_ref, gk_ref, w_ref, o_ref, km_ref, h_s):
    tm = x_ref.shape[0]
    h_s[...] = _rms(x_ref[...], gain_ref[...]).astype(BF16)
    cos = cos_ref[...]
    sin = sin_ref[...]
    half = RET_QK_DIM // 2
    for c in range(D_IN // PROJ_COL_CHUNK):
        c0 = c * PROJ_COL_CHUNK
        acc = jnp.dot(h_s[...], w_ref[:, c0:c0 + PROJ_COL_CHUNK], preferred_element_type=F32)
        if c0 < OFF_VR:
            k_scale = RET_QK_DIM ** -0.5 if c0 >= OFF_KR else None
            for s in range(PROJ_COL_CHUNK // RET_QK_DIM):
                seg = acc[:, s * RET_QK_DIM:(s + 1) * RET_QK_DIM]
                seg = seg * cos + pltpu.roll(seg, half, axis=1) * sin
                if k_scale is not None:
                    seg = seg * k_scale
                o_ref[:, c0 + s * RET_QK_DIM:c0 + (s + 1) * RET_QK_DIM] = seg.astype(BF16)
        elif OFF_QM <= c0 < OFF_VM:
            is_key = c0 >= OFF_KM
            gain = gk_ref[...] if is_key else gq_ref[...]
            for s in range(PROJ_COL_CHUNK // MOBA_HEAD_DIM):
                cols = slice(c0 + s * MOBA_HEAD_DIM, c0 + (s + 1) * MOBA_HEAD_DIM)
                seg = _rms(acc[:, s * MOBA_HEAD_DIM:(s + 1) * MOBA_HEAD_DIM], gain)
                o_ref[:, cols] = seg.astype(BF16)
                if is_key:
                    kcols = slice(cols.start - OFF_KM, cols.stop - OFF_KM)
                    km_ref[:, kcols] = jnp.mean(
                        seg.reshape(tm // MOBA_BLOCK, MOBA_BLOCK, MOBA_HEAD_DIM), axis=1)
        else:
            o_ref[:, c0:c0 + PROJ_COL_CHUNK] = acc.astype(BF16)


def _in_projection(x2, attn_norm, cos, sin, q_gain, k_gain, w_in, seq):
    tokens = x2.shape[0]
    tiles_per_seq = seq // TOKEN_TILE
    blocks_per_tile = TOKEN_TILE // MOBA_BLOCK
    proj, k_mean = pl.pallas_call(
        _inproj_kernel,
        grid=(tokens // TOKEN_TILE,),
        in_specs=[
            pl.BlockSpec((TOKEN_TILE, D_MODEL), lambda i: (i, 0)),
            _resident((1, D_MODEL)),
            pl.BlockSpec((TOKEN_TILE, RET_QK_DIM), lambda i: (i % tiles_per_seq, 0)),
            pl.BlockSpec((TOKEN_TILE, RET_QK_DIM), lambda i: (i % tiles_per_seq, 0)),
            _resident((1, MOBA_HEAD_DIM)),
            _resident((1, MOBA_HEAD_DIM)),
            _resident((D_MODEL, D_IN)),
        ],
        out_specs=[
            pl.BlockSpec((TOKEN_TILE, D_IN), lambda i: (i, 0)),
            pl.BlockSpec((None, blocks_per_tile, MOBA_W), lambda i: (i, 0, 0)),
        ],
        out_shape=[
            jax.ShapeDtypeStruct((tokens, D_IN), BF16),
            jax.ShapeDtypeStruct((tokens // TOKEN_TILE, blocks_per_tile, MOBA_W), F32),
        ],
        scratch_shapes=[pltpu.VMEM((TOKEN_TILE, D_MODEL), BF16)],
        compiler_params=pltpu.CompilerParams(
            dimension_semantics=("arbitrary",), vmem_limit_bytes=V7X_VMEM_LIMIT_BYTES),
        name="in_projection",
    )(x2, attn_norm, cos, sin, q_gain, k_gain, w_in)
    return proj, k_mean.reshape(tokens // MOBA_BLOCK, MOBA_W)


def _retention_kernel(q_ref, k_ref, v_ref, g_ref, dec_ref, xi_ref, zeta_ref, gc_ref,
                      gain_ref, bias_ref, o_ref):
    seq = q_ref.shape[0]
    state = jnp.zeros((RET_QK_DIM, RET_V_DIM), F32)
    for c in range(seq // RET_CHUNK):
        rows = slice(c * RET_CHUNK, (c + 1) * RET_CHUNK)
        qc, kc, vc = q_ref[rows, :], k_ref[rows, :], v_ref[rows, :]
        scores = lax.dot_general(qc, kc, (((1,), (1,)), ((), ())),
                                 preferred_element_type=F32) * dec_ref[...]
        o = jnp.dot(scores.astype(BF16), vc, preferred_element_type=F32)
        o = o + jnp.dot(qc, state.astype(BF16), preferred_element_type=F32) * xi_ref[...]
        kz = (kc.astype(F32) * zeta_ref[...]).astype(BF16)
        state = gc_ref[...] * state + lax.dot_general(
            kz, vc, (((0,), (0,)), ((), ())), preferred_element_type=F32)
        mu = jnp.mean(o, axis=-1, keepdims=True)
        d = o - mu
        var = jnp.mean(d * d, axis=-1, keepdims=True)
        on = d * lax.rsqrt(var + NORM_EPS)
        g = g_ref[rows, :].astype(F32)
        o_ref[rows, :] = ((on * gain_ref[...] + bias_ref[...]) * (g * jax.nn.sigmoid(g))).astype(BF16)


def _retention(proj3, decay, xi, zeta, gamma_c, gain, bias):
    batch, seq, _ = proj3.shape
    qk_blk = lambda off: pl.BlockSpec(
        (None, seq, RET_QK_DIM), lambda b, h: (b, 0, off // RET_QK_DIM + h))
    v_blk = lambda off: pl.BlockSpec(
        (None, seq, RET_V_DIM), lambda b, h: (b, 0, off // RET_V_DIM + h))
    per_head = lambda r, c: pl.BlockSpec((None, r, c), lambda b, h: (h, 0, 0))
    return pl.pallas_call(
        _retention_kernel,
        grid=(batch, RET_HEADS),
        in_specs=[
            qk_blk(OFF_QR), qk_blk(OFF_KR), v_blk(OFF_VR), v_blk(OFF_GR),
            per_head(RET_CHUNK, RET_CHUNK), per_head(RET_CHUNK, RET_V_DIM),
            per_head(RET_CHUNK, RET_QK_DIM), per_head(1, RET_V_DIM),
            pl.BlockSpec((1, RET_V_DIM), lambda b, h: (0, h)),
            pl.BlockSpec((1, RET_V_DIM), lambda b, h: (0, h)),
        ],
        out_specs=pl.BlockSpec((None, seq, RET_V_DIM), lambda b, h: (b, 0, h)),
        out_shape=jax.ShapeDtypeStruct((batch, seq, RET_V), BF16),
        compiler_params=pltpu.CompilerParams(
            dimension_semantics=("arbitrary", "arbitrary"), vmem_limit_bytes=V7X_VMEM_LIMIT_BYTES),
        name="retention",
    )(proj3, proj3, proj3, proj3, decay, xi, zeta, gamma_c, gain, bias)


def _moba_kernel(q_ref, k_ref, v_ref, km_ref, o_ref, km3_s, vt_s, s_s, p_s):
    seq = q_ref.shape[0]
    nb = seq // MOBA_BLOCK
    km = km_ref[...]
    km_hi = km.astype(BF16)
    km_mid = (km - km_hi.astype(F32)).astype(BF16)
    km_lo = (km - km_hi.astype(F32) - km_mid.astype(F32)).astype(BF16)
    km3_s[...] = jnp.concatenate([km_hi, km_mid, km_lo, jnp.zeros_like(km_hi)], axis=0)
    vt_s[...] = v_ref[...].astype(F32).T.astype(BF16)

    key_i = lax.broadcasted_iota(jnp.int32, (MOBA_BLOCK, MOBA_BLOCK), 0)
    qry_i = lax.broadcasted_iota(jnp.int32, (MOBA_BLOCK, MOBA_BLOCK), 1)
    causal = key_i <= qry_i

    def blk(j):
        return slice(j * MOBA_BLOCK, (j + 1) * MOBA_BLOCK)

    def keep_masks(qb):
        if qb <= MOBA_TOPK:
            return [None] * qb
        parts = lax.dot_general(km3_s[...], q_ref[blk(qb), :], (((1,), (1,)), ((), ())),
                                preferred_element_type=F32)
        gate = parts[0:nb] + (parts[nb:2 * nb] + parts[2 * nb:3 * nb])
        g = [gate[i:i + 1, :] for i in range(qb)]
        keep = []
        for j in range(qb):
            beaten = jnp.zeros((1, MOBA_BLOCK), jnp.int32)
            for i in range(qb):
                if i < j:
                    beaten = beaten + (g[i] >= g[j]).astype(jnp.int32)
                elif i > j:
                    beaten = beaten + (g[i] > g[j]).astype(jnp.int32)
            keep.append(beaten < MOBA_TOPK)
        return keep

    def scores_stage(qb):
        slot = qb % 2
        keep = keep_masks(qb)
        q_blk = q_ref[blk(qb), :]
        m = None
        for j in range(qb + 1):
            s = lax.dot_general(k_ref[blk(j), :], q_blk, (((1,), (1,)), ((), ())),
                                preferred_element_type=F32)
            if j == qb:
                s = jnp.where(causal, s, -jnp.inf)
            elif keep[j] is not None:
                s = jnp.where(keep[j], s, -jnp.inf)
            s_s[slot, blk(j), :] = s
            mj = s.max(axis=0, keepdims=True)
            m = mj if m is None else jnp.maximum(m, mj)
        return m

    def softmax_stage(qb, m):
        slot = qb % 2
        denom = jnp.zeros((1, MOBA_BLOCK), F32)
        for j in range(qb + 1):
            p = jnp.exp2(s_s[slot, blk(j), :] - m)
            denom = denom + p.sum(axis=0, keepdims=True)
            p_s[slot, blk(j), :] = p.astype(BF16)
        return denom

    def value_stage(qb, denom):
        slot = qb % 2
        nk = (qb + 1) * MOBA_BLOCK
        o_t = jnp.dot(vt_s[:, 0:nk], p_s[slot, 0:nk, :], preferred_element_type=F32)
        o_ref[blk(qb), :] = (o_t / denom).T.astype(BF16)

    m_next = scores_stage(0)
    for qb in range(nb):
        m_cur = m_next
        if qb + 1 < nb:
            m_next = scores_stage(qb + 1)
        value_stage(qb, softmax_stage(qb, m_cur))


def _moba(proj3, k_mean):
    batch, seq, _ = proj3.shape
    nb = seq // MOBA_BLOCK
    head_blk = lambda off: pl.BlockSpec(
        (None, seq, MOBA_HEAD_DIM), lambda b, h: (b, 0, off // MOBA_HEAD_DIM + h))
    return pl.pallas_call(
        _moba_kernel,
        grid=(batch, MOBA_HEADS),
        in_specs=[
            head_blk(OFF_QM), head_blk(OFF_KM), head_blk(OFF_VM),
            pl.BlockSpec((nb, MOBA_HEAD_DIM), lambda b, h: (b, h)),
        ],
        out_specs=pl.BlockSpec((None, seq, MOBA_HEAD_DIM), lambda b, h: (b, 0, h)),
        out_shape=jax.ShapeDtypeStruct((batch, seq, MOBA_W), BF16),
        scratch_shapes=[
            pltpu.VMEM((4 * nb, MOBA_HEAD_DIM), BF16),
            pltpu.VMEM((MOBA_HEAD_DIM, seq), BF16),
            pltpu.VMEM((2, seq, MOBA_BLOCK), F32),
            pltpu.VMEM((2, seq, MOBA_BLOCK), BF16),
        ],
        compiler_params=pltpu.CompilerParams(
            dimension_semantics=("arbitrary", "arbitrary"), vmem_limit_bytes=V7X_VMEM_LIMIT_BYTES),
        name="block_attention",
    )(proj3, proj3, proj3, k_mean)


def _mix_ffn_kernel(x_ref, or_ref, om_ref, an_ref, wg_ref, bg_ref, wro_ref, wmo_ref, wo_ref,
                    fn_ref, wup_ref, cw_ref, cb_ref, wdn_ref, out_ref,
                    h_s, merged_s, act_s, ubuf_s, carry_s, *, tiles_per_seq):
    tm = x_ref.shape[0]
    halo = V7X_SUBLANES

    @pl.when(pl.program_id(0) % tiles_per_seq == 0)
    def _():
        carry_s[...] = jnp.zeros_like(carry_s)

    x = x_ref[...]
    h_s[...] = _rms(x, an_ref[...]).astype(BF16)
    for c in range(D_MODEL // MIX_COL_CHUNK):
        cs = slice(c * MIX_COL_CHUNK, (c + 1) * MIX_COL_CHUNK)
        cs2 = slice(D_MODEL + c * MIX_COL_CHUNK, D_MODEL + (c + 1) * MIX_COL_CHUNK)
        y_ret = jnp.dot(or_ref[...], wro_ref[:, cs], preferred_element_type=F32)
        y_moba = jnp.dot(om_ref[...], wmo_ref[:, cs], preferred_element_type=F32)
        g_ret = jax.nn.sigmoid(
            jnp.dot(h_s[...], wg_ref[:, cs], preferred_element_type=F32) + bg_ref[:, cs])
        g_moba = jax.nn.sigmoid(
            jnp.dot(h_s[...], wg_ref[:, cs2], preferred_element_type=F32) + bg_ref[:, cs2])
        merged_s[:, cs] = (g_ret * y_ret + g_moba * y_moba).astype(BF16)
    x1 = x + jnp.dot(merged_s[...], wo_ref[...], preferred_element_type=F32)
    out_ref[...] = x1
    h_s[...] = _rms(x1, fn_ref[...]).astype(BF16)

    def conv(u, slot, cols):
        ubuf_s[slot, 0:halo, :] = carry_s[:, cols]
        ubuf_s[slot, halo:halo + tm, :] = u
        carry_s[:, cols] = u[tm - halo:tm, :]
        u1 = ubuf_s[slot, halo - 1:halo - 1 + tm, :]
        u2 = ubuf_s[slot, halo - 2:halo - 2 + tm, :]
        return (cb_ref[:, cols] + cw_ref[0:1, cols] * u2 + cw_ref[1:2, cols] * u1
                + cw_ref[2:3, cols] * u)

    for c in range(D_FF // FF_CHUNK):
        gcols = slice(c * FF_CHUNK, (c + 1) * FF_CHUNK)
        vcols = slice(D_FF + c * FF_CHUNK, D_FF + (c + 1) * FF_CHUNK)
        u_gate = conv(jnp.dot(h_s[...], wup_ref[:, gcols], preferred_element_type=F32), 0, gcols)
        u_val = conv(jnp.dot(h_s[...], wup_ref[:, vcols], preferred_element_type=F32), 1, vcols)
        act_s[:, gcols] = (u_gate * jax.nn.sigmoid(u_gate) * u_val).astype(BF16)
    out_ref[...] += jnp.dot(act_s[...], wdn_ref[...], preferred_element_type=F32)


def _mix_ffn(x2, o_r, o_m, attn_norm, w_gate, b_gate, w_ret_o, w_moba_o, w_out,
             ffn_norm, w_up, conv_w, conv_b, w_down, seq):
    tokens = x2.shape[0]
    tile = lambda cols: pl.BlockSpec((TOKEN_TILE, cols), lambda i: (i, 0))
    return pl.pallas_call(
        functools.partial(_mix_ffn_kernel, tiles_per_seq=seq // TOKEN_TILE),
        grid=(tokens // TOKEN_TILE,),
        in_specs=[
            tile(D_MODEL), tile(RET_V), tile(MOBA_W),
            _resident((1, D_MODEL)), _resident((D_MODEL, 2 * D_MODEL)), _resident((1, 2 * D_MODEL)),
            _resident((RET_V, D_MODEL)), _resident((MOBA_W, D_MODEL)), _resident((D_MODEL, D_MODEL)),
            _resident((1, D_MODEL)), _resident((D_MODEL, 2 * D_FF)),
            _resident((CONV_WIDTH, 2 * D_FF)), _resident((1, 2 * D_FF)), _resident((D_FF, D_MODEL)),
        ],
        out_specs=tile(D_MODEL),
        out_shape=jax.ShapeDtypeStruct((tokens, D_MODEL), F32),
        scratch_shapes=[
            pltpu.VMEM((TOKEN_TILE, D_MODEL), BF16),
            pltpu.VMEM((TOKEN_TILE, D_MODEL), BF16),
            pltpu.VMEM((TOKEN_TILE, D_FF), BF16),
            pltpu.VMEM((2, TOKEN_TILE + V7X_SUBLANES, FF_CHUNK), F32),
            pltpu.VMEM((V7X_SUBLANES, 2 * D_FF), F32),
        ],
        compiler_params=pltpu.CompilerParams(
            dimension_semantics=("arbitrary",), vmem_limit_bytes=V7X_VMEM_LIMIT_BYTES),
        name="merge_ffn",
    )(x2, o_r, o_m, attn_norm, w_gate, b_gate, w_ret_o, w_moba_o, w_out,
      ffn_norm, w_up, conv_w, conv_b, w_down)


def _rotary_tables(seq):
    half = RET_QK_DIM // 2
    inv_freq = ROPE_BASE ** (-jnp.arange(half, dtype=F32) / half)
    ang = jnp.arange(seq).astype(F32)[:, None] * inv_freq[None, :]
    cos, sin = jnp.cos(ang), jnp.sin(ang)
    return jnp.concatenate([cos, cos], axis=-1), jnp.concatenate([-sin, sin], axis=-1)


def _retention_tables():
    log_gamma = jnp.log1p(-jnp.power(2.0, -5.0 - jnp.arange(RET_HEADS, dtype=F32)))
    pos = jnp.arange(RET_CHUNK, dtype=F32)
    diff = pos[:, None] - pos[None, :]
    decay = jnp.where(diff >= 0, jnp.exp(log_gamma[:, None, None] * jnp.maximum(diff, 0.0)), 0.0)
    zeta = jnp.exp(log_gamma[:, None] * (RET_CHUNK - 1 - pos)[None, :])
    xi = jnp.exp(log_gamma[:, None] * (pos + 1)[None, :])
    gamma_c = jnp.exp(log_gamma * RET_CHUNK)
    xi_b = jnp.broadcast_to(xi[:, :, None], (RET_HEADS, RET_CHUNK, RET_V_DIM))
    zeta_b = jnp.broadcast_to(zeta[:, :, None], (RET_HEADS, RET_CHUNK, RET_QK_DIM))
    gamma_b = jnp.broadcast_to(gamma_c[:, None, None], (RET_HEADS, 1, RET_V_DIM))
    return decay, xi_b, zeta_b, gamma_b


def kernel(x, attn_norm, w_in, ret_norm_gain, ret_norm_bias, moba_q_gain, moba_k_gain,
           w_ret_o, w_moba_o, w_gate, b_gate, w_out, ffn_norm, w_up, conv_w, conv_b, w_down):
    batch, seq, d_model = x.shape
    depth = attn_norm.shape[0]
    assert d_model == D_MODEL and seq % TOKEN_TILE == 0 and seq % MOBA_BLOCK == 0
    cos, sin = _rotary_tables(seq)
    decay, xi_b, zeta_b, gamma_b = _retention_tables()
    x2 = x.reshape(batch * seq, d_model)
    for l in range(depth):
        q_gain = moba_q_gain[l][None] * (MOBA_HEAD_DIM ** -0.5 * LOG2_E)
        proj, k_mean = _in_projection(x2, attn_norm[l][None], cos, sin, q_gain,
                                      moba_k_gain[l][None], w_in[l].astype(BF16), seq)
        proj3 = proj.reshape(batch, seq, D_IN)
        o_r = _retention(proj3, decay, xi_b, zeta_b, gamma_b,
                         ret_norm_gain[l][None], ret_norm_bias[l][None])
        o_m = _moba(proj3, k_mean)
        x2 = _mix_ffn(
            x2, o_r.reshape(batch * seq, RET_V), o_m.reshape(batch * seq, MOBA_W),
            attn_norm[l][None], w_gate[l].astype(BF16), b_gate[l][None],
            w_ret_o[l].astype(BF16), w_moba_o[l].astype(BF16), w_out[l].astype(BF16),
            ffn_norm[l][None], w_up[l].astype(BF16), conv_w[l], conv_b[l][None],
            w_down[l].astype(BF16), seq)
    return x2.reshape(batch, seq, d_model)
```

```python
import functools

import jax
import jax.numpy as jnp
from jax import lax
from jax.experimental import pallas as pl
from jax.experimental.pallas import tpu as pltpu

F32 = jnp.float32
BF16 = jnp.bfloat16

D_MODEL = 1024
RET_HEADS = 4
RET_QK_DIM = 128
RET_V_DIM = 256
ROPE_BASE = 10000.0
MOBA_HEADS = 8
MOBA_HEAD_DIM = 128
MOBA_BLOCK = 256
MOBA_TOPK = 3
D_FF = 2816
CONV_WIDTH = 3
NORM_EPS = 1e-6
LOG2_E = 1.4426950408889634

RET_QK = RET_HEADS * RET_QK_DIM
RET_V = RET_HEADS * RET_V_DIM
MOBA_W = MOBA_HEADS * MOBA_HEAD_DIM
D_IN = 2 * RET_QK + 2 * RET_V + 3 * MOBA_W
OFF_QR, OFF_KR, OFF_VR, OFF_GR = 0, RET_QK, 2 * RET_QK, 2 * RET_QK + RET_V
OFF_QM = 2 * RET_QK + 2 * RET_V
OFF_KM, OFF_VM = OFF_QM + MOBA_W, OFF_QM + 2 * MOBA_W

V7X_LANES = 128
V7X_SUBLANES = 8
V7X_VMEM_LIMIT_BYTES = 56 * 1024 * 1024

TOKEN_TILE = 512
PROJ_COL_CHUNK = 512
MIX_COL_CHUNK = 512
FF_CHUNK = 256
MOBA_HEADS_PER_STEP = 2
RET_CHUNK = 256


def _resident(shape):
    nd = len(shape)
    return pl.BlockSpec(shape, lambda *_: (0,) * nd, pipeline_mode=pl.Buffered(1))


def _rms(x, gain):
    inv = lax.rsqrt(jnp.mean(x * x, axis=-1, keepdims=True) + NORM_EPS)
    return (x * inv) * gain


def _inproj_kernel(x_ref, gain_ref, cos_ref, sin_ref, gq_ref, gk_ref, w_ref,
                   o_ref, km_ref, vt_ref, h_s):
    tm = x_ref.shape[0]
    h_s[...] = _rms(x_ref[...], gain_ref[...]).astype(BF16)
    cos = cos_ref[...]
    sin = sin_ref[...]
    half = RET_QK_DIM // 2
    for c in range(D_IN // PROJ_COL_CHUNK):
        c0 = c * PROJ_COL_CHUNK
        acc = jnp.dot(h_s[...], w_ref[:, c0:c0 + PROJ_COL_CHUNK], preferred_element_type=F32)
        if c0 < OFF_VR:
            k_scale = RET_QK_DIM ** -0.5 if c0 >= OFF_KR else None
            for s in range(PROJ_COL_CHUNK // RET_QK_DIM):
                seg = acc[:, s * RET_QK_DIM:(s + 1) * RET_QK_DIM]
                seg = seg * cos + pltpu.roll(seg, half, axis=1) * sin
                if k_scale is not None:
                    seg = seg * k_scale
                o_ref[:, c0 + s * RET_QK_DIM:c0 + (s + 1) * RET_QK_DIM] = seg.astype(BF16)
        elif OFF_QM <= c0 < OFF_VM:
            is_key = c0 >= OFF_KM
            gain = gk_ref[...] if is_key else gq_ref[...]
            for s in range(PROJ_COL_CHUNK // MOBA_HEAD_DIM):
                cols = slice(c0 + s * MOBA_HEAD_DIM, c0 + (s + 1) * MOBA_HEAD_DIM)
                seg = _rms(acc[:, s * MOBA_HEAD_DIM:(s + 1) * MOBA_HEAD_DIM], gain)
                o_ref[:, cols] = seg.astype(BF16)
                if is_key:
                    kcols = slice(cols.start - OFF_KM, cols.stop - OFF_KM)
                    km_ref[:, kcols] = jnp.mean(
                        seg.reshape(tm // MOBA_BLOCK, MOBA_BLOCK, MOBA_HEAD_DIM), axis=1)
        elif c0 >= OFF_VM:
            for s in range(PROJ_COL_CHUNK // MOBA_HEAD_DIM):
                head = (c0 - OFF_VM) // MOBA_HEAD_DIM + s
                seg = acc[:, s * MOBA_HEAD_DIM:(s + 1) * MOBA_HEAD_DIM]
                vt_ref[head, :, :] = seg.T.astype(BF16)
        else:
            o_ref[:, c0:c0 + PROJ_COL_CHUNK] = acc.astype(BF16)


def _in_projection(x2, attn_norm, cos, sin, q_gain, k_gain, w_in, seq):
    tokens = x2.shape[0]
    tiles_per_seq = seq // TOKEN_TILE
    blocks_per_tile = TOKEN_TILE // MOBA_BLOCK
    proj, k_mean, v_t = pl.pallas_call(
        _inproj_kernel,
        grid=(tokens // TOKEN_TILE,),
        in_specs=[
            pl.BlockSpec((TOKEN_TILE, D_MODEL), lambda i: (i, 0)),
            _resident((1, D_MODEL)),
            pl.BlockSpec((TOKEN_TILE, RET_QK_DIM), lambda i: (i % tiles_per_seq, 0)),
            pl.BlockSpec((TOKEN_TILE, RET_QK_DIM), lambda i: (i % tiles_per_seq, 0)),
            _resident((1, MOBA_HEAD_DIM)),
            _resident((1, MOBA_HEAD_DIM)),
            _resident((D_MODEL, D_IN)),
        ],
        out_specs=[
            pl.BlockSpec((TOKEN_TILE, OFF_VM), lambda i: (i, 0)),
            pl.BlockSpec((None, blocks_per_tile, MOBA_W), lambda i: (i, 0, 0)),
            pl.BlockSpec((MOBA_HEADS, MOBA_HEAD_DIM, TOKEN_TILE), lambda i: (0, 0, i)),
        ],
        out_shape=[
            jax.ShapeDtypeStruct((tokens, OFF_VM), BF16),
            jax.ShapeDtypeStruct((tokens // TOKEN_TILE, blocks_per_tile, MOBA_W), F32),
            jax.ShapeDtypeStruct((MOBA_HEADS, MOBA_HEAD_DIM, tokens), BF16),
        ],
        scratch_shapes=[pltpu.VMEM((TOKEN_TILE, D_MODEL), BF16)],
        compiler_params=pltpu.CompilerParams(
            dimension_semantics=("arbitrary",), vmem_limit_bytes=V7X_VMEM_LIMIT_BYTES),
        name="in_projection",
    )(x2, attn_norm, cos, sin, q_gain, k_gain, w_in)
    return proj, k_mean.reshape(tokens // MOBA_BLOCK, MOBA_W), v_t


def _retention_kernel(q_ref, k_ref, v_ref, g_ref, dec_ref, xi_ref, zeta_ref, gc_ref,
                      gain_ref, bias_ref, o_ref):
    seq = q_ref.shape[0]
    state = jnp.zeros((RET_QK_DIM, RET_V_DIM), F32)
    for c in range(seq // RET_CHUNK):
        rows = slice(c * RET_CHUNK, (c + 1) * RET_CHUNK)
        qc, kc, vc = q_ref[rows, :], k_ref[rows, :], v_ref[rows, :]
        scores = lax.dot_general(qc, kc, (((1,), (1,)), ((), ())),
                                 preferred_element_type=F32) * dec_ref[...]
        o = jnp.dot(scores.astype(BF16), vc, preferred_element_type=F32)
        o = o + jnp.dot(qc, state.astype(BF16), preferred_element_type=F32) * xi_ref[...]
        kz = (kc.astype(F32) * zeta_ref[...]).astype(BF16)
        state = gc_ref[...] * state + lax.dot_general(
            kz, vc, (((0,), (0,)), ((), ())), preferred_element_type=F32)
        mu = jnp.mean(o, axis=-1, keepdims=True)
        d = o - mu
        var = jnp.mean(d * d, axis=-1, keepdims=True)
        on = d * lax.rsqrt(var + NORM_EPS)
        g = g_ref[rows, :].astype(F32)
        o_ref[rows, :] = ((on * gain_ref[...] + bias_ref[...]) * (g * jax.nn.sigmoid(g))).astype(BF16)


def _retention(proj3, decay, xi, zeta, gamma_c, gain, bias):
    batch, seq, _ = proj3.shape
    qk_blk = lambda off: pl.BlockSpec(
        (None, seq, RET_QK_DIM), lambda b, h: (b, 0, off // RET_QK_DIM + h))
    v_blk = lambda off: pl.BlockSpec(
        (None, seq, RET_V_DIM), lambda b, h: (b, 0, off // RET_V_DIM + h))
    per_head = lambda r, c: pl.BlockSpec((None, r, c), lambda b, h: (h, 0, 0))
    return pl.pallas_call(
        _retention_kernel,
        grid=(batch, RET_HEADS),
        in_specs=[
            qk_blk(OFF_QR), qk_blk(OFF_KR), v_blk(OFF_VR), v_blk(OFF_GR),
            per_head(RET_CHUNK, RET_CHUNK), per_head(RET_CHUNK, RET_V_DIM),
            per_head(RET_CHUNK, RET_QK_DIM), per_head(1, RET_V_DIM),
            pl.BlockSpec((1, RET_V_DIM), lambda b, h: (0, h)),
            pl.BlockSpec((1, RET_V_DIM), lambda b, h: (0, h)),
        ],
        out_specs=pl.BlockSpec((None, seq, RET_V_DIM), lambda b, h: (b, 0, h)),
        out_shape=jax.ShapeDtypeStruct((batch, seq, RET_V), BF16),
        compiler_params=pltpu.CompilerParams(
            dimension_semantics=("arbitrary", "arbitrary"), vmem_limit_bytes=V7X_VMEM_LIMIT_BYTES),
        name="retention",
    )(proj3, proj3, proj3, proj3, decay, xi, zeta, gamma_c, gain, bias)


def _moba_kernel(q_ref, k_ref, vt_ref, km_ref, o_ref, km3_s, s_s, p_s):
    seq = q_ref.shape[0]
    n_heads = vt_ref.shape[0]
    nb = seq // MOBA_BLOCK
    km = km_ref[...]
    km_hi = km.astype(BF16)
    km_mid = (km - km_hi.astype(F32)).astype(BF16)
    km_lo = (km - km_hi.astype(F32) - km_mid.astype(F32)).astype(BF16)
    km3_s[...] = jnp.concatenate([km_hi, km_mid, km_lo, jnp.zeros_like(km_hi)], axis=0)

    key_i = lax.broadcasted_iota(jnp.int32, (MOBA_BLOCK, MOBA_BLOCK), 0)
    qry_i = lax.broadcasted_iota(jnp.int32, (MOBA_BLOCK, MOBA_BLOCK), 1)
    causal = key_i <= qry_i

    def blk(j):
        return slice(j * MOBA_BLOCK, (j + 1) * MOBA_BLOCK)

    def hcols(h):
        return slice(h * MOBA_HEAD_DIM, (h + 1) * MOBA_HEAD_DIM)

    def keep_masks(h, qb):
        if qb <= MOBA_TOPK:
            return [None] * qb
        parts = lax.dot_general(km3_s[:, hcols(h)], q_ref[blk(qb), hcols(h)],
                                (((1,), (1,)), ((), ())),
                                preferred_element_type=F32)
        gate = parts[0:nb] + (parts[nb:2 * nb] + parts[2 * nb:3 * nb])
        g = [gate[i:i + 1, :] for i in range(qb)]
        keep = []
        for j in range(qb):
            beaten = jnp.zeros((1, MOBA_BLOCK), jnp.int32)
            for i in range(qb):
                if i < j:
                    beaten = beaten + (g[i] >= g[j]).astype(jnp.int32)
                elif i > j:
                    beaten = beaten + (g[i] > g[j]).astype(jnp.int32)
            keep.append(beaten < MOBA_TOPK)
        return keep

    def scores_stage(h, qb):
        slot = qb % 2
        keep = keep_masks(h, qb)
        q_blk = q_ref[blk(qb), hcols(h)]
        m = None
        for j in range(qb + 1):
            s = lax.dot_general(k_ref[blk(j), hcols(h)], q_blk, (((1,), (1,)), ((), ())),
                                preferred_element_type=F32)
            if j == qb:
                s = jnp.where(causal, s, -jnp.inf)
            elif keep[j] is not None:
                s = jnp.where(keep[j], s, -jnp.inf)
            s_s[h, slot, blk(j), :] = s
            mj = s.max(axis=0, keepdims=True)
            m = mj if m is None else jnp.maximum(m, mj)
        return m

    def softmax_stage(h, qb, m):
        slot = qb % 2
        denom = jnp.zeros((1, MOBA_BLOCK), F32)
        for j in range(qb + 1):
            p = jnp.exp2(s_s[h, slot, blk(j), :] - m)
            denom = denom + p.sum(axis=0, keepdims=True)
            p_s[h, slot, blk(j), :] = p.astype(BF16)
        return denom

    def value_stage(h, qb, denom):
        slot = qb % 2
        nk = (qb + 1) * MOBA_BLOCK
        o_t = jnp.dot(vt_ref[h, :, 0:nk], p_s[h, slot, 0:nk, :],
                      preferred_element_type=F32)
        o_ref[blk(qb), hcols(h)] = (o_t / denom).T.astype(BF16)

    heads = range(n_heads)
    m_next = [scores_stage(h, 0) for h in heads]
    for qb in range(nb):
        m_cur = m_next
        if qb + 1 < nb:
            m_next = [scores_stage(h, qb + 1) for h in heads]
        for h in heads:
            value_stage(h, qb, softmax_stage(h, qb, m_cur[h]))


def _moba(proj3, k_mean, v_t):
    batch, seq, _ = proj3.shape
    nb = seq // MOBA_BLOCK
    hps = MOBA_HEADS_PER_STEP
    width = hps * MOBA_HEAD_DIM
    head_blk = lambda off: pl.BlockSpec((None, seq, width), lambda b, g: (b, 0, off // width + g))
    return pl.pallas_call(
        _moba_kernel,
        grid=(batch, MOBA_HEADS // hps),
        in_specs=[
            head_blk(OFF_QM), head_blk(OFF_KM),
            pl.BlockSpec((hps, MOBA_HEAD_DIM, seq), lambda b, g: (g, 0, b)),
            pl.BlockSpec((nb, width), lambda b, g: (b, g)),
        ],
        out_specs=pl.BlockSpec((None, seq, width), lambda b, g: (b, 0, g)),
        out_shape=jax.ShapeDtypeStruct((batch, seq, MOBA_W), BF16),
        scratch_shapes=[
            pltpu.VMEM((4 * nb, width), BF16),
            pltpu.VMEM((hps, 2, seq, MOBA_BLOCK), F32),
            pltpu.VMEM((hps, 2, seq, MOBA_BLOCK), BF16),
        ],
        compiler_params=pltpu.CompilerParams(
            dimension_semantics=("arbitrary", "arbitrary"), vmem_limit_bytes=V7X_VMEM_LIMIT_BYTES),
        name="block_attention",
    )(proj3, proj3, v_t, k_mean)


def _mix_ffn_kernel(x_ref, or_ref, om_ref, an_ref, wg_ref, bg_ref, wro_ref, wmo_ref, wo_ref,
                    fn_ref, wup_ref, cw_ref, cb_ref, wdn_ref, out_ref,
                    h_s, merged_s, act_s, ubuf_s, carry_s, *, tiles_per_seq):
    tm = x_ref.shape[0]
    halo = V7X_SUBLANES

    @pl.when(pl.program_id(0) % tiles_per_seq == 0)
    def _():
        carry_s[...] = jnp.zeros_like(carry_s)

    x = x_ref[...]
    h_s[...] = _rms(x, an_ref[...]).astype(BF16)
    for c in range(D_MODEL // MIX_COL_CHUNK):
        cs = slice(c * MIX_COL_CHUNK, (c + 1) * MIX_COL_CHUNK)
        cs2 = slice(D_MODEL + c * MIX_COL_CHUNK, D_MODEL + (c + 1) * MIX_COL_CHUNK)
        y_ret = jnp.dot(or_ref[...], wro_ref[:, cs], preferred_element_type=F32)
        y_moba = jnp.dot(om_ref[...], wmo_ref[:, cs], preferred_element_type=F32)
        g_ret = jax.nn.sigmoid(
            jnp.dot(h_s[...], wg_ref[:, cs], preferred_element_type=F32) + bg_ref[:, cs])
        g_moba = jax.nn.sigmoid(
            jnp.dot(h_s[...], wg_ref[:, cs2], preferred_element_type=F32) + bg_ref[:, cs2])
        merged_s[:, cs] = (g_ret * y_ret + g_moba * y_moba).astype(BF16)
    x1 = x + jnp.dot(merged_s[...], wo_ref[...], preferred_element_type=F32)
    out_ref[...] = x1
    h_s[...] = _rms(x1, fn_ref[...]).astype(BF16)

    def conv(u, slot, cols):
        ubuf_s[slot, 0:halo, :] = carry_s[:, cols]
        ubuf_s[slot, halo:halo + tm, :] = u
        carry_s[:, cols] = u[tm - halo:tm, :]
        u1 = ubuf_s[slot, halo - 1:halo - 1 + tm, :]
        u2 = ubuf_s[slot, halo - 2:halo - 2 + tm, :]
        return (cb_ref[:, cols] + cw_ref[0:1, cols] * u2 + cw_ref[1:2, cols] * u1
                + cw_ref[2:3, cols] * u)

    n_ff = D_FF // FF_CHUNK
    head = (n_ff - 1) * FF_CHUNK
    for c in range(n_ff):
        gcols = slice(c * FF_CHUNK, (c + 1) * FF_CHUNK)
        vcols = slice(D_FF + c * FF_CHUNK, D_FF + (c + 1) * FF_CHUNK)
        d_gate = jnp.dot(h_s[...], wup_ref[:, gcols], preferred_element_type=F32)
        d_val = jnp.dot(h_s[...], wup_ref[:, vcols], preferred_element_type=F32)
        if c == n_ff - 1:
            out_ref[...] += jnp.dot(act_s[:, 0:head], wdn_ref[0:head, :],
                                    preferred_element_type=F32)
        u_gate = conv(d_gate, 0, gcols)
        u_val = conv(d_val, 1, vcols)
        act_s[:, gcols] = (u_gate * jax.nn.sigmoid(u_gate) * u_val).astype(BF16)
    out_ref[...] += jnp.dot(act_s[:, head:D_FF], wdn_ref[head:D_FF, :], preferred_element_type=F32)


def _mix_ffn(x2, o_r, o_m, attn_norm, w_gate, b_gate, w_ret_o, w_moba_o, w_out,
             ffn_norm, w_up, conv_w, conv_b, w_down, seq):
    tokens = x2.shape[0]
    tile = lambda cols: pl.BlockSpec((TOKEN_TILE, cols), lambda i: (i, 0))
    return pl.pallas_call(
        functools.partial(_mix_ffn_kernel, tiles_per_seq=seq // TOKEN_TILE),
        grid=(tokens // TOKEN_TILE,),
        in_specs=[
            tile(D_MODEL), tile(RET_V), tile(MOBA_W),
            _resident((1, D_MODEL)), _resident((D_MODEL, 2 * D_MODEL)), _resident((1, 2 * D_MODEL)),
            _resident((RET_V, D_MODEL)), _resident((MOBA_W, D_MODEL)), _resident((D_MODEL, D_MODEL)),
            _resident((1, D_MODEL)), _resident((D_MODEL, 2 * D_FF)),
            _resident((CONV_WIDTH, 2 * D_FF)), _resident((1, 2 * D_FF)), _resident((D_FF, D_MODEL)),
        ],
        out_specs=tile(D_MODEL),
        out_shape=jax.ShapeDtypeStruct((tokens, D_MODEL), F32),
        scratch_shapes=[
            pltpu.VMEM((TOKEN_TILE, D_MODEL), BF16),
            pltpu.VMEM((TOKEN_TILE, D_MODEL), BF16),
            pltpu.VMEM((TOKEN_TILE, D_FF), BF16),
            pltpu.VMEM((2, TOKEN_TILE + V7X_SUBLANES, FF_CHUNK), F32),
            pltpu.VMEM((V7X_SUBLANES, 2 * D_FF), F32),
        ],
        compiler_params=pltpu.CompilerParams(
            dimension_semantics=("arbitrary",), vmem_limit_bytes=V7X_VMEM_LIMIT_BYTES),
        name="merge_ffn",
    )(x2, o_r, o_m, attn_norm, w_gate, b_gate, w_ret_o, w_moba_o, w_out,
      ffn_norm, w_up, conv_w, conv_b, w_down)


def _rotary_tables(seq):
    half = RET_QK_DIM // 2
    inv_freq = ROPE_BASE ** (-jnp.arange(half, dtype=F32) / half)
    ang = jnp.arange(seq).astype(F32)[:, None] * inv_freq[None, :]
    cos, sin = jnp.cos(ang), jnp.sin(ang)
    return jnp.concatenate([cos, cos], axis=-1), jnp.concatenate([-sin, sin], axis=-1)


def _retention_tables():
    log_gamma = jnp.log1p(-jnp.power(2.0, -5.0 - jnp.arange(RET_HEADS, dtype=F32)))
    pos = jnp.arange(RET_CHUNK, dtype=F32)
    diff = pos[:, None] - pos[None, :]
    decay = jnp.where(diff >= 0, jnp.exp(log_gamma[:, None, None] * jnp.maximum(diff, 0.0)), 0.0)
    zeta = jnp.exp(log_gamma[:, None] * (RET_CHUNK - 1 - pos)[None, :])
    xi = jnp.exp(log_gamma[:, None] * (pos + 1)[None, :])
    gamma_c = jnp.exp(log_gamma * RET_CHUNK)
    xi_b = jnp.broadcast_to(xi[:, :, None], (RET_HEADS, RET_CHUNK, RET_V_DIM))
    zeta_b = jnp.broadcast_to(zeta[:, :, None], (RET_HEADS, RET_CHUNK, RET_QK_DIM))
    gamma_b = jnp.broadcast_to(gamma_c[:, None, None], (RET_HEADS, 1, RET_V_DIM))
    return decay, xi_b, zeta_b, gamma_b


def kernel(x, attn_norm, w_in, ret_norm_gain, ret_norm_bias, moba_q_gain, moba_k_gain,
           w_ret_o, w_moba_o, w_gate, b_gate, w_out, ffn_norm, w_up, conv_w, conv_b, w_down):
    batch, seq, d_model = x.shape
    depth = attn_norm.shape[0]
    assert d_model == D_MODEL and seq % TOKEN_TILE == 0 and seq % MOBA_BLOCK == 0
    cos, sin = _rotary_tables(seq)
    decay, xi_b, zeta_b, gamma_b = _retention_tables()
    x2 = x.reshape(batch * seq, d_model)
    for l in range(depth):
        q_gain = moba_q_gain[l][None] * (MOBA_HEAD_DIM ** -0.5 * LOG2_E)
        proj, k_mean, v_t = _in_projection(x2, attn_norm[l][None], cos, sin, q_gain,
                                           moba_k_gain[l][None], w_in[l].astype(BF16), seq)
        proj3 = proj.reshape(batch, seq, OFF_VM)
        o_r = _retention(proj3, decay, xi_b, zeta_b, gamma_b,
                         ret_norm_gain[l][None], ret_norm_bias[l][None])
        o_m = _moba(proj3, k_mean, v_t)
        x2 = _mix_ffn(
            x2, o_r.reshape(batch * seq, RET_V), o_m.reshape(batch * seq, MOBA_W),
            attn_norm[l][None], w_gate[l].astype(BF16), b_gate[l][None],
            w_ret_o[l].astype(BF16), w_moba_o[l].astype(BF16), w_out[l].astype(BF16),
            ffn_norm[l][None], w_up[l].astype(BF16), conv_w[l], conv_b[l][None],
            w_down[l].astype(BF16), seq)
    return x2.reshape(batch, seq, d_model)
```

```python
import functools

import jax
import jax.numpy as jnp
from jax import lax
from jax.experimental import pallas as pl
from jax.experimental.pallas import tpu as pltpu

F32 = jnp.float32
BF16 = jnp.bfloat16

D_MODEL = 1024
RET_HEADS = 4
RET_QK_DIM = 128
RET_V_DIM = 256
ROPE_BASE = 10000.0
MOBA_HEADS = 8
MOBA_HEAD_DIM = 128
MOBA_BLOCK = 256
MOBA_TOPK = 3
D_FF = 2816
CONV_WIDTH = 3
NORM_EPS = 1e-6
LOG2_E = 1.4426950408889634

RET_QK = RET_HEADS * RET_QK_DIM
RET_V = RET_HEADS * RET_V_DIM
MOBA_W = MOBA_HEADS * MOBA_HEAD_DIM
D_IN = 2 * RET_QK + 2 * RET_V + 3 * MOBA_W
OFF_QR, OFF_KR, OFF_VR, OFF_GR = 0, RET_QK, 2 * RET_QK, 2 * RET_QK + RET_V
OFF_QM = 2 * RET_QK + 2 * RET_V
OFF_KM, OFF_VM = OFF_QM + MOBA_W, OFF_QM + 2 * MOBA_W

V7X_LANES = 128
V7X_SUBLANES = 8
V7X_VMEM_LIMIT_BYTES = 56 * 1024 * 1024

TOKEN_TILE = 512
PROJ_COL_CHUNK = 512
MIX_COL_CHUNK = 512
FF_CHUNK = 256
MOBA_HEADS_PER_STEP = 2
MOBA_VT_ROWS = MOBA_HEAD_DIM + 16
RET_HEADS_PER_STEP = 2
RET_CHUNK = 256


def _resident(shape):
    nd = len(shape)
    return pl.BlockSpec(shape, lambda *_: (0,) * nd, pipeline_mode=pl.Buffered(1))


def _rms(x, gain):
    inv = lax.rsqrt(jnp.mean(x * x, axis=-1, keepdims=True) + NORM_EPS)
    return (x * inv) * gain


def _inproj_kernel(x_ref, gain_ref, cos_ref, sin_ref, gq_ref, gk_ref, w_ref,
                   o_ref, km_ref, vt_ref, h_s):
    tm = x_ref.shape[0]
    h_s[...] = _rms(x_ref[...], gain_ref[...]).astype(BF16)
    cos = cos_ref[...]
    sin = sin_ref[...]
    half = RET_QK_DIM // 2
    for c in range(D_IN // PROJ_COL_CHUNK):
        c0 = c * PROJ_COL_CHUNK
        acc = jnp.dot(h_s[...], w_ref[:, c0:c0 + PROJ_COL_CHUNK], preferred_element_type=F32)
        if c0 < OFF_VR:
            k_scale = RET_QK_DIM ** -0.5 if c0 >= OFF_KR else None
            for s in range(PROJ_COL_CHUNK // RET_QK_DIM):
                seg = acc[:, s * RET_QK_DIM:(s + 1) * RET_QK_DIM]
                seg = seg * cos + pltpu.roll(seg, half, axis=1) * sin
                if k_scale is not None:
                    seg = seg * k_scale
                o_ref[:, c0 + s * RET_QK_DIM:c0 + (s + 1) * RET_QK_DIM] = seg.astype(BF16)
        elif OFF_QM <= c0 < OFF_VM:
            is_key = c0 >= OFF_KM
            gain = gk_ref[...] if is_key else gq_ref[...]
            for s in range(PROJ_COL_CHUNK // MOBA_HEAD_DIM):
                cols = slice(c0 + s * MOBA_HEAD_DIM, c0 + (s + 1) * MOBA_HEAD_DIM)
                seg = _rms(acc[:, s * MOBA_HEAD_DIM:(s + 1) * MOBA_HEAD_DIM], gain)
                o_ref[:, cols] = seg.astype(BF16)
                if is_key:
                    kcols = slice(cols.start - OFF_KM, cols.stop - OFF_KM)
                    km_ref[:, kcols] = jnp.mean(
                        seg.reshape(tm // MOBA_BLOCK, MOBA_BLOCK, MOBA_HEAD_DIM), axis=1)
        elif c0 >= OFF_VM:
            for s in range(PROJ_COL_CHUNK // MOBA_HEAD_DIM):
                head = (c0 - OFF_VM) // MOBA_HEAD_DIM + s
                seg = acc[:, s * MOBA_HEAD_DIM:(s + 1) * MOBA_HEAD_DIM]
                vt_ref[head, 0:MOBA_HEAD_DIM, :] = seg.T.astype(BF16)
                vt_ref[head, MOBA_HEAD_DIM:MOBA_VT_ROWS, :] = jnp.ones(
                    (MOBA_VT_ROWS - MOBA_HEAD_DIM, tm), BF16)
        else:
            o_ref[:, c0:c0 + PROJ_COL_CHUNK] = acc.astype(BF16)


def _in_projection(x2, attn_norm, cos, sin, q_gain, k_gain, w_in, seq):
    tokens = x2.shape[0]
    tiles_per_seq = seq // TOKEN_TILE
    blocks_per_tile = TOKEN_TILE // MOBA_BLOCK
    proj, k_mean, v_t = pl.pallas_call(
        _inproj_kernel,
        grid=(tokens // TOKEN_TILE,),
        in_specs=[
            pl.BlockSpec((TOKEN_TILE, D_MODEL), lambda i: (i, 0)),
            _resident((1, D_MODEL)),
            pl.BlockSpec((TOKEN_TILE, RET_QK_DIM), lambda i: (i % tiles_per_seq, 0)),
            pl.BlockSpec((TOKEN_TILE, RET_QK_DIM), lambda i: (i % tiles_per_seq, 0)),
            _resident((1, MOBA_HEAD_DIM)),
            _resident((1, MOBA_HEAD_DIM)),
            _resident((D_MODEL, D_IN)),
        ],
        out_specs=[
            pl.BlockSpec((TOKEN_TILE, OFF_VM), lambda i: (i, 0)),
            pl.BlockSpec((None, blocks_per_tile, MOBA_W), lambda i: (i, 0, 0)),
            pl.BlockSpec((MOBA_HEADS, MOBA_VT_ROWS, TOKEN_TILE), lambda i: (0, 0, i)),
        ],
        out_shape=[
            jax.ShapeDtypeStruct((tokens, OFF_VM), BF16),
            jax.ShapeDtypeStruct((tokens // TOKEN_TILE, blocks_per_tile, MOBA_W), F32),
            jax.ShapeDtypeStruct((MOBA_HEADS, MOBA_VT_ROWS, tokens), BF16),
        ],
        scratch_shapes=[pltpu.VMEM((TOKEN_TILE, D_MODEL), BF16)],
        compiler_params=pltpu.CompilerParams(
            dimension_semantics=("arbitrary",), vmem_limit_bytes=V7X_VMEM_LIMIT_BYTES),
        name="in_projection",
    )(x2, attn_norm, cos, sin, q_gain, k_gain, w_in)
    return proj, k_mean.reshape(tokens // MOBA_BLOCK, MOBA_W), v_t


def _retention_kernel(q_ref, k_ref, v_ref, g_ref, dec_ref, xi_ref, zeta_ref, gc_ref,
                      gain_ref, bias_ref, o_ref):
    seq = q_ref.shape[0]
    n_heads = dec_ref.shape[0]
    state = [jnp.zeros((RET_QK_DIM, RET_V_DIM), F32) for _ in range(n_heads)]
    for c in range(seq // RET_CHUNK):
        rows = slice(c * RET_CHUNK, (c + 1) * RET_CHUNK)
        for h in range(n_heads):
            qk_cols = slice(h * RET_QK_DIM, (h + 1) * RET_QK_DIM)
            v_cols = slice(h * RET_V_DIM, (h + 1) * RET_V_DIM)
            qc, kc, vc = q_ref[rows, qk_cols], k_ref[rows, qk_cols], v_ref[rows, v_cols]
            scores = lax.dot_general(qc, kc, (((1,), (1,)), ((), ())),
                                     preferred_element_type=F32) * dec_ref[h]
            o = jnp.dot(scores.astype(BF16), vc, preferred_element_type=F32)
            o = o + jnp.dot(qc, state[h].astype(BF16), preferred_element_type=F32) * xi_ref[h]
            kz = (kc.astype(F32) * zeta_ref[h]).astype(BF16)
            state[h] = gc_ref[h] * state[h] + lax.dot_general(
                kz, vc, (((0,), (0,)), ((), ())), preferred_element_type=F32)
            mu = jnp.mean(o, axis=-1, keepdims=True)
            d = o - mu
            var = jnp.mean(d * d, axis=-1, keepdims=True)
            on = d * lax.rsqrt(var + NORM_EPS)
            g = g_ref[rows, v_cols].astype(F32)
            o_ref[rows, v_cols] = ((on * gain_ref[:, v_cols] + bias_ref[:, v_cols])
                                   * (g * jax.nn.sigmoid(g))).astype(BF16)


def _retention(proj3, decay, xi, zeta, gamma_c, gain, bias):
    batch, seq, _ = proj3.shape
    hps = RET_HEADS_PER_STEP
    qk_w, v_w = hps * RET_QK_DIM, hps * RET_V_DIM
    qk_blk = lambda off: pl.BlockSpec((None, seq, qk_w), lambda b, g: (b, 0, off // qk_w + g))
    v_blk = lambda off: pl.BlockSpec((None, seq, v_w), lambda b, g: (b, 0, off // v_w + g))
    per_head = lambda r, c: pl.BlockSpec((hps, r, c), lambda b, g: (g, 0, 0))
    return pl.pallas_call(
        _retention_kernel,
        grid=(batch, RET_HEADS // hps),
        in_specs=[
            qk_blk(OFF_QR), qk_blk(OFF_KR), v_blk(OFF_VR), v_blk(OFF_GR),
            per_head(RET_CHUNK, RET_CHUNK), per_head(RET_CHUNK, RET_V_DIM),
            per_head(RET_CHUNK, RET_QK_DIM), per_head(1, RET_V_DIM),
            pl.BlockSpec((1, v_w), lambda b, g: (0, g)),
            pl.BlockSpec((1, v_w), lambda b, g: (0, g)),
        ],
        out_specs=pl.BlockSpec((None, seq, v_w), lambda b, g: (b, 0, g)),
        out_shape=jax.ShapeDtypeStruct((batch, seq, RET_V), BF16),
        compiler_params=pltpu.CompilerParams(
            dimension_semantics=("arbitrary", "arbitrary"), vmem_limit_bytes=V7X_VMEM_LIMIT_BYTES),
        name="retention",
    )(proj3, proj3, proj3, proj3, decay, xi, zeta, gamma_c, gain, bias)


def _moba_kernel(q_ref, k_ref, vt_ref, km_ref, o_ref, km3_s, s_s, p_s):
    seq = q_ref.shape[0]
    n_heads = vt_ref.shape[0]
    nb = seq // MOBA_BLOCK
    km = km_ref[...]
    km_hi = km.astype(BF16)
    km_mid = (km - km_hi.astype(F32)).astype(BF16)
    km_lo = (km - km_hi.astype(F32) - km_mid.astype(F32)).astype(BF16)
    km3_s[...] = jnp.concatenate([km_hi, km_mid, km_lo, jnp.zeros_like(km_hi)], axis=0)

    key_i = lax.broadcasted_iota(jnp.int32, (MOBA_BLOCK, MOBA_BLOCK), 0)
    qry_i = lax.broadcasted_iota(jnp.int32, (MOBA_BLOCK, MOBA_BLOCK), 1)
    causal = key_i <= qry_i

    def blk(j):
        return slice(j * MOBA_BLOCK, (j + 1) * MOBA_BLOCK)

    def hcols(h):
        return slice(h * MOBA_HEAD_DIM, (h + 1) * MOBA_HEAD_DIM)

    def keep_masks(h, qb):
        if qb <= MOBA_TOPK:
            return [None] * qb
        parts = lax.dot_general(km3_s[:, hcols(h)], q_ref[blk(qb), hcols(h)],
                                (((1,), (1,)), ((), ())),
                                preferred_element_type=F32)
        gate = parts[0:nb] + (parts[nb:2 * nb] + parts[2 * nb:3 * nb])
        g = [gate[i:i + 1, :] for i in range(qb)]
        keep = []
        for j in range(qb):
            beaten = jnp.zeros((1, MOBA_BLOCK), jnp.int32)
            for i in range(qb):
                if i < j:
                    beaten = beaten + (g[i] >= g[j]).astype(jnp.int32)
                elif i > j:
                    beaten = beaten + (g[i] > g[j]).astype(jnp.int32)
            keep.append(beaten < MOBA_TOPK)
        return keep

    def scores_stage(h, qb):
        slot = qb % 2
        keep = keep_masks(h, qb)
        q_blk = q_ref[blk(qb), hcols(h)]
        m = None
        for j in range(qb + 1):
            s = lax.dot_general(k_ref[blk(j), hcols(h)], q_blk, (((1,), (1,)), ((), ())),
                                preferred_element_type=F32)
            if j == qb:
                s = jnp.where(causal, s, -jnp.inf)
            elif keep[j] is not None:
                s = jnp.where(keep[j], s, -jnp.inf)
            s_s[h, slot, blk(j), :] = s
            mj = s_s[h, slot, blk(j), :].max(axis=0, keepdims=True)
            m = mj if m is None else jnp.maximum(m, mj)
        return m

    def softmax_stage(h, qb, m):
        slot = qb % 2
        for j in range(qb + 1):
            p_s[h, slot, blk(j), :] = jnp.exp2(s_s[h, slot, blk(j), :] - m).astype(BF16)

    def value_stage(h, qb):
        slot = qb % 2
        nk = (qb + 1) * MOBA_BLOCK
        o_t = jnp.dot(vt_ref[h, :, 0:nk], p_s[h, slot, 0:nk, :],
                      preferred_element_type=F32)
        denom = o_t[MOBA_HEAD_DIM:MOBA_HEAD_DIM + 1, :]
        o_ref[blk(qb), hcols(h)] = (o_t[0:MOBA_HEAD_DIM, :] / denom).T.astype(BF16)

    heads = range(n_heads)
    m_next = [scores_stage(h, 0) for h in heads]
    for qb in range(nb):
        m_cur = m_next
        if qb + 1 < nb:
            m_next = [scores_stage(h, qb + 1) for h in heads]
        for h in heads:
            softmax_stage(h, qb, m_cur[h])
            value_stage(h, qb)


def _moba(proj3, k_mean, v_t):
    batch, seq, _ = proj3.shape
    nb = seq // MOBA_BLOCK
    hps = MOBA_HEADS_PER_STEP
    width = hps * MOBA_HEAD_DIM
    head_blk = lambda off: pl.BlockSpec((None, seq, width), lambda b, g: (b, 0, off // width + g))
    return pl.pallas_call(
        _moba_kernel,
        grid=(batch, MOBA_HEADS // hps),
        in_specs=[
            head_blk(OFF_QM), head_blk(OFF_KM),
            pl.BlockSpec((hps, MOBA_VT_ROWS, seq), lambda b, g: (g, 0, b)),
            pl.BlockSpec((nb, width), lambda b, g: (b, g)),
        ],
        out_specs=pl.BlockSpec((None, seq, width), lambda b, g: (b, 0, g)),
        out_shape=jax.ShapeDtypeStruct((batch, seq, MOBA_W), BF16),
        scratch_shapes=[
            pltpu.VMEM((4 * nb, width), BF16),
            pltpu.VMEM((hps, 2, seq, MOBA_BLOCK), F32),
            pltpu.VMEM((hps, 2, seq, MOBA_BLOCK), BF16),
        ],
        compiler_params=pltpu.CompilerParams(
            dimension_semantics=("arbitrary", "arbitrary"), vmem_limit_bytes=V7X_VMEM_LIMIT_BYTES),
        name="block_attention",
    )(proj3, proj3, v_t, k_mean)


def _mix_ffn_kernel(x_ref, or_ref, om_ref, an_ref, wg_ref, bg_ref, wro_ref, wmo_ref, wo_ref,
                    fn_ref, wup_ref, cw_ref, cb_ref, wdn_ref, out_ref,
                    h_s, merged_s, act_s, ubuf_s, carry_s, *, tiles_per_seq):
    tm = x_ref.shape[0]
    halo = V7X_SUBLANES

    @pl.when(pl.program_id(0) % tiles_per_seq == 0)
    def _():
        carry_s[...] = jnp.zeros_like(carry_s)

    x = x_ref[...]
    h_s[...] = _rms(x, an_ref[...]).astype(BF16)
    for c in range(D_MODEL // MIX_COL_CHUNK):
        cs = slice(c * MIX_COL_CHUNK, (c + 1) * MIX_COL_CHUNK)
        cs2 = slice(D_MODEL + c * MIX_COL_CHUNK, D_MODEL + (c + 1) * MIX_COL_CHUNK)
        y_ret = jnp.dot(or_ref[...], wro_ref[:, cs], preferred_element_type=F32)
        y_moba = jnp.dot(om_ref[...], wmo_ref[:, cs], preferred_element_type=F32)
        g_ret = jax.nn.sigmoid(
            jnp.dot(h_s[...], wg_ref[:, cs], preferred_element_type=F32) + bg_ref[:, cs])
        g_moba = jax.nn.sigmoid(
            jnp.dot(h_s[...], wg_ref[:, cs2], preferred_element_type=F32) + bg_ref[:, cs2])
        merged_s[:, cs] = (g_ret * y_ret + g_moba * y_moba).astype(BF16)
    x1 = x + jnp.dot(merged_s[...], wo_ref[...], preferred_element_type=F32)
    out_ref[...] = x1
    h_s[...] = _rms(x1, fn_ref[...]).astype(BF16)

    def conv(u, slot, cols):
        ubuf_s[slot, 0:halo, :] = carry_s[:, cols]
        ubuf_s[slot, halo:halo + tm, :] = u
        carry_s[:, cols] = u[tm - halo:tm, :]
        u1 = ubuf_s[slot, halo - 1:halo - 1 + tm, :]
        u2 = ubuf_s[slot, halo - 2:halo - 2 + tm, :]
        return (cb_ref[:, cols] + cw_ref[0:1, cols] * u2 + cw_ref[1:2, cols] * u1
                + cw_ref[2:3, cols] * u)

    n_ff = D_FF // FF_CHUNK
    head = (n_ff - 1) * FF_CHUNK
    for c in range(n_ff):
        gcols = slice(c * FF_CHUNK, (c + 1) * FF_CHUNK)
        vcols = slice(D_FF + c * FF_CHUNK, D_FF + (c + 1) * FF_CHUNK)
        d_gate = jnp.dot(h_s[...], wup_ref[:, gcols], preferred_element_type=F32)
        d_val = jnp.dot(h_s[...], wup_ref[:, vcols], preferred_element_type=F32)
        if c == n_ff - 1:
            out_ref[...] += jnp.dot(act_s[:, 0:head], wdn_ref[0:head, :],
                                    preferred_element_type=F32)
        u_gate = conv(d_gate, 0, gcols)
        u_val = conv(d_val, 1, vcols)
        act_s[:, gcols] = (u_gate * jax.nn.sigmoid(u_gate) * u_val).astype(BF16)
    out_ref[...] += jnp.dot(act_s[:, head:D_FF], wdn_ref[head:D_FF, :], preferred_element_type=F32)


def _mix_ffn(x2, o_r, o_m, attn_norm, w_gate, b_gate, w_ret_o, w_moba_o, w_out,
             ffn_norm, w_up, conv_w, conv_b, w_down, seq):
    tokens = x2.shape[0]
    tile = lambda cols: pl.BlockSpec((TOKEN_TILE, cols), lambda i: (i, 0))
    return pl.pallas_call(
        functools.partial(_mix_ffn_kernel, tiles_per_seq=seq // TOKEN_TILE),
        grid=(tokens // TOKEN_TILE,),
        in_specs=[
            tile(D_MODEL), tile(RET_V), tile(MOBA_W),
            _resident((1, D_MODEL)), _resident((D_MODEL, 2 * D_MODEL)), _resident((1, 2 * D_MODEL)),
            _resident((RET_V, D_MODEL)), _resident((MOBA_W, D_MODEL)), _resident((D_MODEL, D_MODEL)),
            _resident((1, D_MODEL)), _resident((D_MODEL, 2 * D_FF)),
            _resident((CONV_WIDTH, 2 * D_FF)), _resident((1, 2 * D_FF)), _resident((D_FF, D_MODEL)),
        ],
        out_specs=tile(D_MODEL),
        out_shape=jax.ShapeDtypeStruct((tokens, D_MODEL), F32),
        scratch_shapes=[
            pltpu.VMEM((TOKEN_TILE, D_MODEL), BF16),
            pltpu.VMEM((TOKEN_TILE, D_MODEL), BF16),
            pltpu.VMEM((TOKEN_TILE, D_FF), BF16),
            pltpu.VMEM((2, TOKEN_TILE + V7X_SUBLANES, FF_CHUNK), F32),
            pltpu.VMEM((V7X_SUBLANES, 2 * D_FF), F32),
        ],
        compiler_params=pltpu.CompilerParams(
            dimension_semantics=("arbitrary",), vmem_limit_bytes=V7X_VMEM_LIMIT_BYTES),
        name="merge_ffn",
    )(x2, o_r, o_m, attn_norm, w_gate, b_gate, w_ret_o, w_moba_o, w_out,
      ffn_norm, w_up, conv_w, conv_b, w_down)


def _rotary_tables(seq):
    half = RET_QK_DIM // 2
    inv_freq = ROPE_BASE ** (-jnp.arange(half, dtype=F32) / half)
    ang = jnp.arange(seq).astype(F32)[:, None] * inv_freq[None, :]
    cos, sin = jnp.cos(ang), jnp.sin(ang)
    return jnp.concatenate([cos, cos], axis=-1), jnp.concatenate([-sin, sin], axis=-1)


def _retention_tables():
    log_gamma = jnp.log1p(-jnp.power(2.0, -5.0 - jnp.arange(RET_HEADS, dtype=F32)))
    pos = jnp.arange(RET_CHUNK, dtype=F32)
    diff = pos[:, None] - pos[None, :]
    decay = jnp.where(diff >= 0, jnp.exp(log_gamma[:, None, None] * jnp.maximum(diff, 0.0)), 0.0)
    zeta = jnp.exp(log_gamma[:, None] * (RET_CHUNK - 1 - pos)[None, :])
    xi = jnp.exp(log_gamma[:, None] * (pos + 1)[None, :])
    gamma_c = jnp.exp(log_gamma * RET_CHUNK)
    xi_b = jnp.broadcast_to(xi[:, :, None], (RET_HEADS, RET_CHUNK, RET_V_DIM))
    zeta_b = jnp.broadcast_to(zeta[:, :, None], (RET_HEADS, RET_CHUNK, RET_QK_DIM))
    gamma_b = jnp.broadcast_to(gamma_c[:, None, None], (RET_HEADS, 1, RET_V_DIM))
    return decay, xi_b, zeta_b, gamma_b


def kernel(x, attn_norm, w_in, ret_norm_gain, ret_norm_bias, moba_q_gain, moba_k_gain,
           w_ret_o, w_moba_o, w_gate, b_gate, w_out, ffn_norm, w_up, conv_w, conv_b, w_down):
    batch, seq, d_model = x.shape
    depth = attn_norm.shape[0]
    assert d_model == D_MODEL and seq % TOKEN_TILE == 0 and seq % MOBA_BLOCK == 0
    cos, sin = _rotary_tables(seq)
    decay, xi_b, zeta_b, gamma_b = _retention_tables()
    x2 = x.reshape(batch * seq, d_model)
    for l in range(depth):
        q_gain = moba_q_gain[l][None] * (MOBA_HEAD_DIM ** -0.5 * LOG2_E)
        proj, k_mean, v_t = _in_projection(x2, attn_norm[l][None], cos, sin, q_gain,
                                           moba_k_gain[l][None], w_in[l].astype(BF16), seq)
        proj3 = proj.reshape(batch, seq, OFF_VM)
        o_r = _retention(proj3, decay, xi_b, zeta_b, gamma_b,
                         ret_norm_gain[l][None], ret_norm_bias[l][None])
        o_m = _moba(proj3, k_mean, v_t)
        x2 = _mix_ffn(
            x2, o_r.reshape(batch * seq, RET_V), o_m.reshape(batch * seq, MOBA_W),
            attn_norm[l][None], w_gate[l].astype(BF16), b_gate[l][None],
            w_ret_o[l].astype(BF16), w_moba_o[l].astype(BF16), w_out[l].astype(BF16),
            ffn_norm[l][None], w_up[l].astype(BF16), conv_w[l], conv_b[l][None],
            w_down[l].astype(BF16), seq)
    return x2.reshape(batch, seq, d_model)
```

```python
import functools

import jax
import jax.numpy as jnp
from jax import lax
from jax.experimental import pallas as pl
from jax.experimental.pallas import tpu as pltpu

F32 = jnp.float32
BF16 = jnp.bfloat16

D_MODEL = 1024
RET_HEADS = 4
RET_QK_DIM = 128
RET_V_DIM = 256
ROPE_BASE = 10000.0
MOBA_HEADS = 8
MOBA_HEAD_DIM = 128
MOBA_BLOCK = 256
MOBA_TOPK = 3
D_FF = 2816
CONV_WIDTH = 3
NORM_EPS = 1e-6
LOG2_E = 1.4426950408889634

RET_QK = RET_HEADS * RET_QK_DIM
RET_V = RET_HEADS * RET_V_DIM
MOBA_W = MOBA_HEADS * MOBA_HEAD_DIM
D_IN = 2 * RET_QK + 2 * RET_V + 3 * MOBA_W
OFF_QR, OFF_KR, OFF_VR, OFF_GR = 0, RET_QK, 2 * RET_QK, 2 * RET_QK + RET_V
OFF_QM = 2 * RET_QK + 2 * RET_V
OFF_KM, OFF_VM = OFF_QM + MOBA_W, OFF_QM + 2 * MOBA_W
PROJ_OFF_KM = OFF_QM
PROJ_W = PROJ_OFF_KM + MOBA_W

V7X_LANES = 128
V7X_SUBLANES = 8
V7X_VMEM_LIMIT_BYTES = 56 * 1024 * 1024

TOKEN_TILE = 512
PROJ_COL_CHUNK = 512
MIX_COL_CHUNK = 512
FF_CHUNK = 256
MOBA_HEADS_PER_STEP = 2
MOBA_VT_ROWS = MOBA_HEAD_DIM + 16
RET_HEADS_PER_STEP = 2
RET_CHUNK = 256


def _resident(shape):
    nd = len(shape)
    return pl.BlockSpec(shape, lambda *_: (0,) * nd, pipeline_mode=pl.Buffered(1))


def _rms(x, gain):
    inv = lax.rsqrt(jnp.mean(x * x, axis=-1, keepdims=True) + NORM_EPS)
    return (x * inv) * gain


def _inproj_kernel(x_ref, gain_ref, cos_ref, sin_ref, gq_ref, gk_ref, w_ref,
                   o_ref, km_ref, vt_ref, qt_ref, h_s):
    tm = x_ref.shape[0]
    h_s[...] = _rms(x_ref[...], gain_ref[...]).astype(BF16)
    cos = cos_ref[...]
    sin = sin_ref[...]
    half = RET_QK_DIM // 2
    for c in range(D_IN // PROJ_COL_CHUNK):
        c0 = c * PROJ_COL_CHUNK
        acc = jnp.dot(h_s[...], w_ref[:, c0:c0 + PROJ_COL_CHUNK], preferred_element_type=F32)
        if c0 < OFF_VR:
            k_scale = RET_QK_DIM ** -0.5 if c0 >= OFF_KR else None
            for s in range(PROJ_COL_CHUNK // RET_QK_DIM):
                seg = acc[:, s * RET_QK_DIM:(s + 1) * RET_QK_DIM]
                seg = seg * cos + pltpu.roll(seg, half, axis=1) * sin
                if k_scale is not None:
                    seg = seg * k_scale
                o_ref[:, c0 + s * RET_QK_DIM:c0 + (s + 1) * RET_QK_DIM] = seg.astype(BF16)
        elif OFF_QM <= c0 < OFF_VM:
            is_key = c0 >= OFF_KM
            gain = gk_ref[...] if is_key else gq_ref[...]
            for s in range(PROJ_COL_CHUNK // MOBA_HEAD_DIM):
                seg = _rms(acc[:, s * MOBA_HEAD_DIM:(s + 1) * MOBA_HEAD_DIM], gain)
                if is_key:
                    head = (c0 - OFF_KM) // MOBA_HEAD_DIM + s
                    kcols = slice(head * MOBA_HEAD_DIM, (head + 1) * MOBA_HEAD_DIM)
                    o_ref[:, PROJ_OFF_KM + kcols.start:PROJ_OFF_KM + kcols.stop] = seg.astype(BF16)
                    km_ref[:, kcols] = jnp.mean(
                        seg.reshape(tm // MOBA_BLOCK, MOBA_BLOCK, MOBA_HEAD_DIM), axis=1)
                else:
                    head = (c0 - OFF_QM) // MOBA_HEAD_DIM + s
                    qt_ref[head, :, :] = seg.T.astype(BF16)
        elif c0 >= OFF_VM:
            for s in range(PROJ_COL_CHUNK // MOBA_HEAD_DIM):
                head = (c0 - OFF_VM) // MOBA_HEAD_DIM + s
                seg = acc[:, s * MOBA_HEAD_DIM:(s + 1) * MOBA_HEAD_DIM]
                vt_ref[head, 0:MOBA_HEAD_DIM, :] = seg.T.astype(BF16)
                vt_ref[head, MOBA_HEAD_DIM:MOBA_VT_ROWS, :] = jnp.ones(
                    (MOBA_VT_ROWS - MOBA_HEAD_DIM, tm), BF16)
        else:
            o_ref[:, c0:c0 + PROJ_COL_CHUNK] = acc.astype(BF16)


def _in_projection(x2, attn_norm, cos, sin, q_gain, k_gain, w_in, seq):
    tokens = x2.shape[0]
    tiles_per_seq = seq // TOKEN_TILE
    blocks_per_tile = TOKEN_TILE // MOBA_BLOCK
    proj, k_mean, v_t, q_t = pl.pallas_call(
        _inproj_kernel,
        grid=(tokens // TOKEN_TILE,),
        in_specs=[
            pl.BlockSpec((TOKEN_TILE, D_MODEL), lambda i: (i, 0)),
            _resident((1, D_MODEL)),
            pl.BlockSpec((TOKEN_TILE, RET_QK_DIM), lambda i: (i % tiles_per_seq, 0)),
            pl.BlockSpec((TOKEN_TILE, RET_QK_DIM), lambda i: (i % tiles_per_seq, 0)),
            _resident((1, MOBA_HEAD_DIM)),
            _resident((1, MOBA_HEAD_DIM)),
            _resident((D_MODEL, D_IN)),
        ],
        out_specs=[
            pl.BlockSpec((TOKEN_TILE, PROJ_W), lambda i: (i, 0)),
            pl.BlockSpec((None, blocks_per_tile, MOBA_W), lambda i: (i, 0, 0)),
            pl.BlockSpec((MOBA_HEADS, MOBA_VT_ROWS, TOKEN_TILE), lambda i: (0, 0, i)),
            pl.BlockSpec((MOBA_HEADS, MOBA_HEAD_DIM, TOKEN_TILE), lambda i: (0, 0, i)),
        ],
        out_shape=[
            jax.ShapeDtypeStruct((tokens, PROJ_W), BF16),
            jax.ShapeDtypeStruct((tokens // TOKEN_TILE, blocks_per_tile, MOBA_W), F32),
            jax.ShapeDtypeStruct((MOBA_HEADS, MOBA_VT_ROWS, tokens), BF16),
            jax.ShapeDtypeStruct((MOBA_HEADS, MOBA_HEAD_DIM, tokens), BF16),
        ],
        scratch_shapes=[pltpu.VMEM((TOKEN_TILE, D_MODEL), BF16)],
        compiler_params=pltpu.CompilerParams(
            dimension_semantics=("arbitrary",), vmem_limit_bytes=V7X_VMEM_LIMIT_BYTES),
        name="in_projection",
    )(x2, attn_norm, cos, sin, q_gain, k_gain, w_in)
    return proj, k_mean.reshape(tokens // MOBA_BLOCK, MOBA_W), v_t, q_t


def _retention_kernel(q_ref, k_ref, v_ref, g_ref, dec_ref, xi_ref, zeta_ref, gc_ref,
                      gain_ref, bias_ref, o_ref):
    seq = q_ref.shape[0]
    n_heads = dec_ref.shape[0]
    state = [jnp.zeros((RET_QK_DIM, RET_V_DIM), F32) for _ in range(n_heads)]
    for c in range(seq // RET_CHUNK):
        rows = slice(c * RET_CHUNK, (c + 1) * RET_CHUNK)
        for h in range(n_heads):
            qk_cols = slice(h * RET_QK_DIM, (h + 1) * RET_QK_DIM)
            v_cols = slice(h * RET_V_DIM, (h + 1) * RET_V_DIM)
            qc, kc, vc = q_ref[rows, qk_cols], k_ref[rows, qk_cols], v_ref[rows, v_cols]
            scores = lax.dot_general(qc, kc, (((1,), (1,)), ((), ())),
                                     preferred_element_type=F32) * dec_ref[h]
            o = jnp.dot(scores.astype(BF16), vc, preferred_element_type=F32)
            o = o + jnp.dot(qc, state[h].astype(BF16), preferred_element_type=F32) * xi_ref[h]
            kz = (kc.astype(F32) * zeta_ref[h]).astype(BF16)
            state[h] = gc_ref[h] * state[h] + lax.dot_general(
                kz, vc, (((0,), (0,)), ((), ())), preferred_element_type=F32)
            mu = jnp.mean(o, axis=-1, keepdims=True)
            d = o - mu
            var = jnp.mean(d * d, axis=-1, keepdims=True)
            on = d * lax.rsqrt(var + NORM_EPS)
            g = g_ref[rows, v_cols].astype(F32)
            o_ref[rows, v_cols] = ((on * gain_ref[:, v_cols] + bias_ref[:, v_cols])
                                   * (g * jax.nn.sigmoid(g))).astype(BF16)


def _retention(proj3, decay, xi, zeta, gamma_c, gain, bias):
    batch, seq, _ = proj3.shape
    hps = RET_HEADS_PER_STEP
    qk_w, v_w = hps * RET_QK_DIM, hps * RET_V_DIM
    qk_blk = lambda off: pl.BlockSpec((None, seq, qk_w), lambda b, g: (b, 0, off // qk_w + g))
    v_blk = lambda off: pl.BlockSpec((None, seq, v_w), lambda b, g: (b, 0, off // v_w + g))
    per_head = lambda r, c: pl.BlockSpec((hps, r, c), lambda b, g: (g, 0, 0))
    return pl.pallas_call(
        _retention_kernel,
        grid=(batch, RET_HEADS // hps),
        in_specs=[
            qk_blk(OFF_QR), qk_blk(OFF_KR), v_blk(OFF_VR), v_blk(OFF_GR),
            per_head(RET_CHUNK, RET_CHUNK), per_head(RET_CHUNK, RET_V_DIM),
            per_head(RET_CHUNK, RET_QK_DIM), per_head(1, RET_V_DIM),
            pl.BlockSpec((1, v_w), lambda b, g: (0, g)),
            pl.BlockSpec((1, v_w), lambda b, g: (0, g)),
        ],
        out_specs=pl.BlockSpec((None, seq, v_w), lambda b, g: (b, 0, g)),
        out_shape=jax.ShapeDtypeStruct((batch, seq, RET_V), BF16),
        compiler_params=pltpu.CompilerParams(
            dimension_semantics=("arbitrary", "arbitrary"), vmem_limit_bytes=V7X_VMEM_LIMIT_BYTES),
        name="retention",
    )(proj3, proj3, proj3, proj3, decay, xi, zeta, gamma_c, gain, bias)


def _moba_kernel(qt_ref, k_ref, vt_ref, km_ref, o_ref, km3_s, s_s, p_s):
    seq = k_ref.shape[0]
    n_heads = vt_ref.shape[0]
    nb = seq // MOBA_BLOCK
    km = km_ref[...]
    km_hi = km.astype(BF16)
    km_mid = (km - km_hi.astype(F32)).astype(BF16)
    km_lo = (km - km_hi.astype(F32) - km_mid.astype(F32)).astype(BF16)
    km3_s[...] = jnp.concatenate([km_hi, km_mid, km_lo, jnp.zeros_like(km_hi)], axis=0)

    key_i = lax.broadcasted_iota(jnp.int32, (MOBA_BLOCK, MOBA_BLOCK), 0)
    qry_i = lax.broadcasted_iota(jnp.int32, (MOBA_BLOCK, MOBA_BLOCK), 1)
    causal = key_i <= qry_i

    def blk(j):
        return slice(j * MOBA_BLOCK, (j + 1) * MOBA_BLOCK)

    def hcols(h):
        return slice(h * MOBA_HEAD_DIM, (h + 1) * MOBA_HEAD_DIM)

    def keep_masks(h, qb):
        if qb <= MOBA_TOPK:
            return [None] * qb
        parts = jnp.dot(km3_s[:, hcols(h)], qt_ref[h, :, blk(qb)],
                        preferred_element_type=F32)
        gate = parts[0:nb] + (parts[nb:2 * nb] + parts[2 * nb:3 * nb])
        g = [gate[i:i + 1, :] for i in range(qb)]
        keep = []
        for j in range(qb):
            beaten = jnp.zeros((1, MOBA_BLOCK), jnp.int32)
            for i in range(qb):
                if i < j:
                    beaten = beaten + (g[i] >= g[j]).astype(jnp.int32)
                elif i > j:
                    beaten = beaten + (g[i] > g[j]).astype(jnp.int32)
            keep.append(beaten < MOBA_TOPK)
        return keep

    def scores_stage(h, qb):
        slot = qb % 2
        keep = keep_masks(h, qb)
        q_blk_t = qt_ref[h, :, blk(qb)]
        m = None
        for j in range(qb + 1):
            s = jnp.dot(k_ref[blk(j), hcols(h)], q_blk_t,
                        preferred_element_type=F32)
            if j == qb:
                s = jnp.where(causal, s, -jnp.inf)
            elif keep[j] is not None:
                s = jnp.where(keep[j], s, -jnp.inf)
            s_s[h, slot, blk(j), :] = s
            mj = s_s[h, slot, blk(j), :].max(axis=0, keepdims=True)
            m = mj if m is None else jnp.maximum(m, mj)
        return m

    def softmax_stage(h, qb, m):
        slot = qb % 2
        for j in range(qb + 1):
            p_s[h, slot, blk(j), :] = jnp.exp2(s_s[h, slot, blk(j), :] - m).astype(BF16)

    def value_stage(h, qb):
        slot = qb % 2
        nk = (qb + 1) * MOBA_BLOCK
        o_t = jnp.dot(vt_ref[h, :, 0:nk], p_s[h, slot, 0:nk, :],
                      preferred_element_type=F32)
        denom = o_t[MOBA_HEAD_DIM:MOBA_HEAD_DIM + 1, :]
        o_ref[blk(qb), hcols(h)] = (o_t[0:MOBA_HEAD_DIM, :] / denom).T.astype(BF16)

    heads = range(n_heads)
    m_next = [scores_stage(h, 0) for h in heads]
    for qb in range(nb):
        m_cur = m_next
        if qb + 1 < nb:
            m_next = [scores_stage(h, qb + 1) for h in heads]
        for h in heads:
            softmax_stage(h, qb, m_cur[h])
            value_stage(h, qb)


def _moba(proj3, k_mean, v_t, q_t):
    batch, seq, _ = proj3.shape
    nb = seq // MOBA_BLOCK
    hps = MOBA_HEADS_PER_STEP
    width = hps * MOBA_HEAD_DIM
    return pl.pallas_call(
        _moba_kernel,
        grid=(batch, MOBA_HEADS // hps),
        in_specs=[
            pl.BlockSpec((hps, MOBA_HEAD_DIM, seq), lambda b, g: (g, 0, b)),
            pl.BlockSpec((None, seq, width), lambda b, g: (b, 0, PROJ_OFF_KM // width + g)),
            pl.BlockSpec((hps, MOBA_VT_ROWS, seq), lambda b, g: (g, 0, b)),
            pl.BlockSpec((nb, width), lambda b, g: (b, g)),
        ],
        out_specs=pl.BlockSpec((None, seq, width), lambda b, g: (b, 0, g)),
        out_shape=jax.ShapeDtypeStruct((batch, seq, MOBA_W), BF16),
        scratch_shapes=[
            pltpu.VMEM((4 * nb, width), BF16),
            pltpu.VMEM((hps, 2, seq, MOBA_BLOCK), F32),
            pltpu.VMEM((hps, 2, seq, MOBA_BLOCK), BF16),
        ],
        compiler_params=pltpu.CompilerParams(
            dimension_semantics=("arbitrary", "arbitrary"), vmem_limit_bytes=V7X_VMEM_LIMIT_BYTES),
        name="block_attention",
    )(q_t, proj3, v_t, k_mean)


def _mix_ffn_kernel(x_ref, or_ref, om_ref, an_ref, wg_ref, bg_ref, wro_ref, wmo_ref, wo_ref,
                    fn_ref, wup_ref, cw_ref, cb_ref, wdn_ref, out_ref,
                    h_s, merged_s, act_s, ubuf_s, carry_s, *, tiles_per_seq):
    tm = x_ref.shape[0]
    halo = V7X_SUBLANES

    @pl.when(pl.program_id(0) % tiles_per_seq == 0)
    def _():
        carry_s[...] = jnp.zeros_like(carry_s)

    x = x_ref[...]
    h_s[...] = _rms(x, an_ref[...]).astype(BF16)
    for c in range(D_MODEL // MIX_COL_CHUNK):
        cs = slice(c * MIX_COL_CHUNK, (c + 1) * MIX_COL_CHUNK)
        cs2 = slice(D_MODEL + c * MIX_COL_CHUNK, D_MODEL + (c + 1) * MIX_COL_CHUNK)
        y_ret = jnp.dot(or_ref[...], wro_ref[:, cs], preferred_element_type=F32)
        y_moba = jnp.dot(om_ref[...], wmo_ref[:, cs], preferred_element_type=F32)
        g_ret = jax.nn.sigmoid(
            jnp.dot(h_s[...], wg_ref[:, cs], preferred_element_type=F32) + bg_ref[:, cs])
        g_moba = jax.nn.sigmoid(
            jnp.dot(h_s[...], wg_ref[:, cs2], preferred_element_type=F32) + bg_ref[:, cs2])
        merged_s[:, cs] = (g_ret * y_ret + g_moba * y_moba).astype(BF16)
    x1 = x + jnp.dot(merged_s[...], wo_ref[...], preferred_element_type=F32)
    out_ref[...] = x1
    h_s[...] = _rms(x1, fn_ref[...]).astype(BF16)

    def conv(u, slot, cols):
        ubuf_s[slot, 0:halo, :] = carry_s[:, cols]
        ubuf_s[slot, halo:halo + tm, :] = u
        carry_s[:, cols] = u[tm - halo:tm, :]
        u1 = ubuf_s[slot, halo - 1:halo - 1 + tm, :]
        u2 = ubuf_s[slot, halo - 2:halo - 2 + tm, :]
        return (cb_ref[:, cols] + cw_ref[0:1, cols] * u2 + cw_ref[1:2, cols] * u1
                + cw_ref[2:3, cols] * u)

    n_ff = D_FF // FF_CHUNK
    head = (n_ff - 1) * FF_CHUNK
    for c in range(n_ff):
        gcols = slice(c * FF_CHUNK, (c + 1) * FF_CHUNK)
        vcols = slice(D_FF + c * FF_CHUNK, D_FF + (c + 1) * FF_CHUNK)
        d_gate = jnp.dot(h_s[...], wup_ref[:, gcols], preferred_element_type=F32)
        d_val = jnp.dot(h_s[...], wup_ref[:, vcols], preferred_element_type=F32)
        if c == n_ff - 1:
            out_ref[...] += jnp.dot(act_s[:, 0:head], wdn_ref[0:head, :],
                                    preferred_element_type=F32)
        u_gate = conv(d_gate, 0, gcols)
        u_val = conv(d_val, 1, vcols)
        act_s[:, gcols] = (u_gate * jax.nn.sigmoid(u_gate) * u_val).astype(BF16)
    out_ref[...] += jnp.dot(act_s[:, head:D_FF], wdn_ref[head:D_FF, :], preferred_element_type=F32)


def _mix_ffn(x2, o_r, o_m, attn_norm, w_gate, b_gate, w_ret_o, w_moba_o, w_out,
             ffn_norm, w_up, conv_w, conv_b, w_down, seq):
    tokens = x2.shape[0]
    tile = lambda cols: pl.BlockSpec((TOKEN_TILE, cols), lambda i: (i, 0))
    return pl.pallas_call(
        functools.partial(_mix_ffn_kernel, tiles_per_seq=seq // TOKEN_TILE),
        grid=(tokens // TOKEN_TILE,),
        in_specs=[
            tile(D_MODEL), tile(RET_V), tile(MOBA_W),
            _resident((1, D_MODEL)), _resident((D_MODEL, 2 * D_MODEL)), _resident((1, 2 * D_MODEL)),
            _resident((RET_V, D_MODEL)), _resident((MOBA_W, D_MODEL)), _resident((D_MODEL, D_MODEL)),
            _resident((1, D_MODEL)), _resident((D_MODEL, 2 * D_FF)),
            _resident((CONV_WIDTH, 2 * D_FF)), _resident((1, 2 * D_FF)), _resident((D_FF, D_MODEL)),
        ],
        out_specs=tile(D_MODEL),
        out_shape=jax.ShapeDtypeStruct((tokens, D_MODEL), F32),
        scratch_shapes=[
            pltpu.VMEM((TOKEN_TILE, D_MODEL), BF16),
            pltpu.VMEM((TOKEN_TILE, D_MODEL), BF16),
            pltpu.VMEM((TOKEN_TILE, D_FF), BF16),
            pltpu.VMEM((2, TOKEN_TILE + V7X_SUBLANES, FF_CHUNK), F32),
            pltpu.VMEM((V7X_SUBLANES, 2 * D_FF), F32),
        ],
        compiler_params=pltpu.CompilerParams(
            dimension_semantics=("arbitrary",), vmem_limit_bytes=V7X_VMEM_LIMIT_BYTES),
        name="merge_ffn",
    )(x2, o_r, o_m, attn_norm, w_gate, b_gate, w_ret_o, w_moba_o, w_out,
      ffn_norm, w_up, conv_w, conv_b, w_down)


def _rotary_tables(seq):
    half = RET_QK_DIM // 2
    inv_freq = ROPE_BASE ** (-jnp.arange(half, dtype=F32) / half)
    ang = jnp.arange(seq).astype(F32)[:, None] * inv_freq[None, :]
    cos, sin = jnp.cos(ang), jnp.sin(ang)
    return jnp.concatenate([cos, cos], axis=-1), jnp.concatenate([-sin, sin], axis=-1)


def _retention_tables():
    log_gamma = jnp.log1p(-jnp.power(2.0, -5.0 - jnp.arange(RET_HEADS, dtype=F32)))
    pos = jnp.arange(RET_CHUNK, dtype=F32)
    diff = pos[:, None] - pos[None, :]
    decay = jnp.where(diff >= 0, jnp.exp(log_gamma[:, None, None] * jnp.maximum(diff, 0.0)), 0.0)
    zeta = jnp.exp(log_gamma[:, None] * (RET_CHUNK - 1 - pos)[None, :])
    xi = jnp.exp(log_gamma[:, None] * (pos + 1)[None, :])
    gamma_c = jnp.exp(log_gamma * RET_CHUNK)
    xi_b = jnp.broadcast_to(xi[:, :, None], (RET_HEADS, RET_CHUNK, RET_V_DIM))
    zeta_b = jnp.broadcast_to(zeta[:, :, None], (RET_HEADS, RET_CHUNK, RET_QK_DIM))
    gamma_b = jnp.broadcast_to(gamma_c[:, None, None], (RET_HEADS, 1, RET_V_DIM))
    return decay, xi_b, zeta_b, gamma_b


def kernel(x, attn_norm, w_in, ret_norm_gain, ret_norm_bias, moba_q_gain, moba_k_gain,
           w_ret_o, w_moba_o, w_gate, b_gate, w_out, ffn_norm, w_up, conv_w, conv_b, w_down):
    batch, seq, d_model = x.shape
    depth = attn_norm.shape[0]
    assert d_model == D_MODEL and seq % TOKEN_TILE == 0 and seq % MOBA_BLOCK == 0
    cos, sin = _rotary_tables(seq)
    decay, xi_b, zeta_b, gamma_b = _retention_tables()
    x2 = x.reshape(batch * seq, d_model)
    for l in range(depth):
        q_gain = moba_q_gain[l][None] * (MOBA_HEAD_DIM ** -0.5 * LOG2_E)
        proj, k_mean, v_t, q_t = _in_projection(x2, attn_norm[l][None], cos, sin, q_gain,
                                                moba_k_gain[l][None], w_in[l].astype(BF16), seq)
        proj3 = proj.reshape(batch, seq, PROJ_W)
        o_r = _retention(proj3, decay, xi_b, zeta_b, gamma_b,
                         ret_norm_gain[l][None], ret_norm_bias[l][None])
        o_m = _moba(proj3, k_mean, v_t, q_t)
        x2 = _mix_ffn(
            x2, o_r.reshape(batch * seq, RET_V), o_m.reshape(batch * seq, MOBA_W),
            attn_norm[l][None], w_gate[l].astype(BF16), b_gate[l][None],
            w_ret_o[l].astype(BF16), w_moba_o[l].astype(BF16), w_out[l].astype(BF16),
            ffn_norm[l][None], w_up[l].astype(BF16), conv_w[l], conv_b[l][None],
            w_down[l].astype(BF16), seq)
    return x2.reshape(batch, seq, d_model)
```

```python
import functools

import jax
import jax.numpy as jnp
from jax import lax
from jax.experimental import pallas as pl
from jax.experimental.pallas import tpu as pltpu

F32 = jnp.float32
BF16 = jnp.bfloat16

D_MODEL = 1024
RET_HEADS = 4
RET_QK_DIM = 128
RET_V_DIM = 256
ROPE_BASE = 10000.0
MOBA_HEADS = 8
MOBA_HEAD_DIM = 128
MOBA_BLOCK = 256
MOBA_TOPK = 3
D_FF = 2816
CONV_WIDTH = 3
NORM_EPS = 1e-6
LOG2_E = 1.4426950408889634

UP_COLS = -(-2 * D_FF // 1024) * 1024

RET_QK = RET_HEADS * RET_QK_DIM
RET_V = RET_HEADS * RET_V_DIM
MOBA_W = MOBA_HEADS * MOBA_HEAD_DIM
D_IN = 2 * RET_QK + 2 * RET_V + 3 * MOBA_W
OFF_QR, OFF_KR, OFF_VR, OFF_GR = 0, RET_QK, 2 * RET_QK, 2 * RET_QK + RET_V
OFF_QM = 2 * RET_QK + 2 * RET_V
OFF_KM, OFF_VM = OFF_QM + MOBA_W, OFF_QM + 2 * MOBA_W
PROJ_OFF_KM = OFF_QM
PROJ_W = PROJ_OFF_KM + MOBA_W

V7X_LANES = 128
V7X_SUBLANES = 8
V7X_VMEM_LIMIT_BYTES = 56 * 1024 * 1024

TOKEN_TILE = 512
PROJ_COL_CHUNK = 512
MIX_COL_CHUNK = 512
FF_CHUNK = 256
MOBA_HEADS_PER_STEP = 2
MOBA_VT_ROWS = MOBA_HEAD_DIM + 16
RET_HEADS_PER_STEP = 2
RET_CHUNK = 256


def _resident(shape):
    nd = len(shape)
    return pl.BlockSpec(shape, lambda *_: (0,) * nd, pipeline_mode=pl.Buffered(1))


def _rms(x, gain):
    inv = lax.rsqrt(jnp.mean(x * x, axis=-1, keepdims=True) + NORM_EPS)
    return (x * inv) * gain


def _inproj_kernel(x_ref, gain_ref, cos_ref, sin_ref, gq_ref, gk_ref, w_ref,
                   o_ref, km_ref, vt_ref, qt_ref, h_s):
    tm = x_ref.shape[0]
    h_s[...] = _rms(x_ref[...], gain_ref[...]).astype(BF16)
    cos = cos_ref[...]
    sin = sin_ref[...]
    half = RET_QK_DIM // 2
    for c in range(D_IN // PROJ_COL_CHUNK):
        c0 = c * PROJ_COL_CHUNK
        acc = jnp.dot(h_s[...], w_ref[:, c0:c0 + PROJ_COL_CHUNK], preferred_element_type=F32)
        if c0 < OFF_VR:
            k_scale = RET_QK_DIM ** -0.5 if c0 >= OFF_KR else None
            for s in range(PROJ_COL_CHUNK // RET_QK_DIM):
                seg = acc[:, s * RET_QK_DIM:(s + 1) * RET_QK_DIM]
                seg = seg * cos + pltpu.roll(seg, half, axis=1) * sin
                if k_scale is not None:
                    seg = seg * k_scale
                o_ref[:, c0 + s * RET_QK_DIM:c0 + (s + 1) * RET_QK_DIM] = seg.astype(BF16)
        elif OFF_QM <= c0 < OFF_VM:
            is_key = c0 >= OFF_KM
            gain = gk_ref[...] if is_key else gq_ref[...]
            for s in range(PROJ_COL_CHUNK // MOBA_HEAD_DIM):
                seg = _rms(acc[:, s * MOBA_HEAD_DIM:(s + 1) * MOBA_HEAD_DIM], gain)
                if is_key:
                    head = (c0 - OFF_KM) // MOBA_HEAD_DIM + s
                    kcols = slice(head * MOBA_HEAD_DIM, (head + 1) * MOBA_HEAD_DIM)
                    o_ref[:, PROJ_OFF_KM + kcols.start:PROJ_OFF_KM + kcols.stop] = seg.astype(BF16)
                    km_ref[:, kcols] = jnp.mean(
                        seg.reshape(tm // MOBA_BLOCK, MOBA_BLOCK, MOBA_HEAD_DIM), axis=1)
                else:
                    head = (c0 - OFF_QM) // MOBA_HEAD_DIM + s
                    qt_ref[head, :, :] = seg.T.astype(BF16)
        elif c0 >= OFF_VM:
            for s in range(PROJ_COL_CHUNK // MOBA_HEAD_DIM):
                head = (c0 - OFF_VM) // MOBA_HEAD_DIM + s
                seg = acc[:, s * MOBA_HEAD_DIM:(s + 1) * MOBA_HEAD_DIM]
                vt_ref[head, 0:MOBA_HEAD_DIM, :] = seg.T.astype(BF16)
                vt_ref[head, MOBA_HEAD_DIM:MOBA_VT_ROWS, :] = jnp.ones(
                    (MOBA_VT_ROWS - MOBA_HEAD_DIM, tm), BF16)
        else:
            o_ref[:, c0:c0 + PROJ_COL_CHUNK] = acc.astype(BF16)


def _in_projection(x2, attn_norm, cos, sin, q_gain, k_gain, w_in, seq):
    tokens = x2.shape[0]
    tiles_per_seq = seq // TOKEN_TILE
    blocks_per_tile = TOKEN_TILE // MOBA_BLOCK
    proj, k_mean, v_t, q_t = pl.pallas_call(
        _inproj_kernel,
        grid=(tokens // TOKEN_TILE,),
        in_specs=[
            pl.BlockSpec((TOKEN_TILE, D_MODEL), lambda i: (i, 0)),
            _resident((1, D_MODEL)),
            pl.BlockSpec((TOKEN_TILE, RET_QK_DIM), lambda i: (i % tiles_per_seq, 0)),
            pl.BlockSpec((TOKEN_TILE, RET_QK_DIM), lambda i: (i % tiles_per_seq, 0)),
            _resident((1, MOBA_HEAD_DIM)),
            _resident((1, MOBA_HEAD_DIM)),
            _resident((D_MODEL, D_IN)),
        ],
        out_specs=[
            pl.BlockSpec((TOKEN_TILE, PROJ_W), lambda i: (i, 0)),
            pl.BlockSpec((None, blocks_per_tile, MOBA_W), lambda i: (i, 0, 0)),
            pl.BlockSpec((MOBA_HEADS, MOBA_VT_ROWS, TOKEN_TILE), lambda i: (0, 0, i)),
            pl.BlockSpec((MOBA_HEADS, MOBA_HEAD_DIM, TOKEN_TILE), lambda i: (0, 0, i)),
        ],
        out_shape=[
            jax.ShapeDtypeStruct((tokens, PROJ_W), BF16),
            jax.ShapeDtypeStruct((tokens // TOKEN_TILE, blocks_per_tile, MOBA_W), F32),
            jax.ShapeDtypeStruct((MOBA_HEADS, MOBA_VT_ROWS, tokens), BF16),
            jax.ShapeDtypeStruct((MOBA_HEADS, MOBA_HEAD_DIM, tokens), BF16),
        ],
        scratch_shapes=[pltpu.VMEM((TOKEN_TILE, D_MODEL), BF16)],
        compiler_params=pltpu.CompilerParams(
            dimension_semantics=("arbitrary",), vmem_limit_bytes=V7X_VMEM_LIMIT_BYTES),
        name="in_projection",
    )(x2, attn_norm, cos, sin, q_gain, k_gain, w_in)
    return proj, k_mean.reshape(tokens // MOBA_BLOCK, MOBA_W), v_t, q_t


def _retention_kernel(q_ref, k_ref, v_ref, g_ref, dec_ref, xi_ref, zeta_ref, gc_ref,
                      gain_ref, bias_ref, o_ref):
    seq = q_ref.shape[0]
    n_heads = dec_ref.shape[0]
    state = [jnp.zeros((RET_QK_DIM, RET_V_DIM), F32) for _ in range(n_heads)]
    for c in range(seq // RET_CHUNK):
        rows = slice(c * RET_CHUNK, (c + 1) * RET_CHUNK)
        for h in range(n_heads):
            qk_cols = slice(h * RET_QK_DIM, (h + 1) * RET_QK_DIM)
            v_cols = slice(h * RET_V_DIM, (h + 1) * RET_V_DIM)
            qc, kc, vc = q_ref[rows, qk_cols], k_ref[rows, qk_cols], v_ref[rows, v_cols]
            scores = lax.dot_general(qc, kc, (((1,), (1,)), ((), ())),
                                     preferred_element_type=F32) * dec_ref[h]
            o = jnp.dot(scores.astype(BF16), vc, preferred_element_type=F32)
            o = o + jnp.dot(qc, state[h].astype(BF16), preferred_element_type=F32) * xi_ref[h]
            kz = (kc.astype(F32) * zeta_ref[h]).astype(BF16)
            state[h] = gc_ref[h] * state[h] + lax.dot_general(
                kz, vc, (((0,), (0,)), ((), ())), preferred_element_type=F32)
            mu = jnp.mean(o, axis=-1, keepdims=True)
            d = o - mu
            var = jnp.mean(d * d, axis=-1, keepdims=True)
            on = d * lax.rsqrt(var + NORM_EPS)
            g = g_ref[rows, v_cols].astype(F32)
            o_ref[rows, v_cols] = ((on * gain_ref[:, v_cols] + bias_ref[:, v_cols])
                                   * (g * jax.nn.sigmoid(g))).astype(BF16)


def _retention(proj3, decay, xi, zeta, gamma_c, gain, bias):
    batch, seq, _ = proj3.shape
    hps = RET_HEADS_PER_STEP
    qk_w, v_w = hps * RET_QK_DIM, hps * RET_V_DIM
    qk_blk = lambda off: pl.BlockSpec((None, seq, qk_w), lambda b, g: (b, 0, off // qk_w + g))
    v_blk = lambda off: pl.BlockSpec((None, seq, v_w), lambda b, g: (b, 0, off // v_w + g))
    per_head = lambda r, c: pl.BlockSpec((hps, r, c), lambda b, g: (g, 0, 0))
    return pl.pallas_call(
        _retention_kernel,
        grid=(batch, RET_HEADS // hps),
        in_specs=[
            qk_blk(OFF_QR), qk_blk(OFF_KR), v_blk(OFF_VR), v_blk(OFF_GR),
            per_head(RET_CHUNK, RET_CHUNK), per_head(RET_CHUNK, RET_V_DIM),
            per_head(RET_CHUNK, RET_QK_DIM), per_head(1, RET_V_DIM),
            pl.BlockSpec((1, v_w), lambda b, g: (0, g)),
            pl.BlockSpec((1, v_w), lambda b, g: (0, g)),
        ],
        out_specs=pl.BlockSpec((None, seq, v_w), lambda b, g: (b, 0, g)),
        out_shape=jax.ShapeDtypeStruct((batch, seq, RET_V), BF16),
        compiler_params=pltpu.CompilerParams(
            dimension_semantics=("arbitrary", "arbitrary"), vmem_limit_bytes=V7X_VMEM_LIMIT_BYTES),
        name="retention",
    )(proj3, proj3, proj3, proj3, decay, xi, zeta, gamma_c, gain, bias)


def _moba_kernel(qt_ref, k_ref, vt_ref, km_ref, o_ref, km3_s, s_s, p_s):
    seq = k_ref.shape[0]
    n_heads = vt_ref.shape[0]
    nb = seq // MOBA_BLOCK
    km = km_ref[...]
    km_hi = km.astype(BF16)
    km_mid = (km - km_hi.astype(F32)).astype(BF16)
    km_lo = (km - km_hi.astype(F32) - km_mid.astype(F32)).astype(BF16)
    km3_s[...] = jnp.concatenate([km_hi, km_mid, km_lo, jnp.zeros_like(km_hi)], axis=0)

    key_i = lax.broadcasted_iota(jnp.int32, (MOBA_BLOCK, MOBA_BLOCK), 0)
    qry_i = lax.broadcasted_iota(jnp.int32, (MOBA_BLOCK, MOBA_BLOCK), 1)
    causal = key_i <= qry_i

    def blk(j):
        return slice(j * MOBA_BLOCK, (j + 1) * MOBA_BLOCK)

    def hcols(h):
        return slice(h * MOBA_HEAD_DIM, (h + 1) * MOBA_HEAD_DIM)

    def keep_masks(h, qb):
        if qb <= MOBA_TOPK:
            return [None] * qb
        parts = jnp.dot(km3_s[:, hcols(h)], qt_ref[h, :, blk(qb)],
                        preferred_element_type=F32)
        gate = parts[0:nb] + (parts[nb:2 * nb] + parts[2 * nb:3 * nb])
        g = [gate[i:i + 1, :] for i in range(qb)]
        keep = []
        for j in range(qb):
            beaten = jnp.zeros((1, MOBA_BLOCK), jnp.int32)
            for i in range(qb):
                if i < j:
                    beaten = beaten + (g[i] >= g[j]).astype(jnp.int32)
                elif i > j:
                    beaten = beaten + (g[i] > g[j]).astype(jnp.int32)
            keep.append(beaten < MOBA_TOPK)
        return keep

    def scores_stage(h, qb):
        slot = qb % 2
        keep = keep_masks(h, qb)
        q_blk_t = qt_ref[h, :, blk(qb)]
        m = None
        for j in range(qb + 1):
            s = jnp.dot(k_ref[blk(j), hcols(h)], q_blk_t,
                        preferred_element_type=F32)
            if j == qb:
                s = jnp.where(causal, s, -jnp.inf)
            elif keep[j] is not None:
                s = jnp.where(keep[j], s, -jnp.inf)
            s_s[h, slot, blk(j), :] = s
            mj = s_s[h, slot, blk(j), :].max(axis=0, keepdims=True)
            m = mj if m is None else jnp.maximum(m, mj)
        return m

    def softmax_stage(h, qb, m):
        slot = qb % 2
        for j in range(qb + 1):
            p_s[h, slot, blk(j), :] = jnp.exp2(s_s[h, slot, blk(j), :] - m).astype(BF16)

    def value_stage(h, qb):
        slot = qb % 2
        nk = (qb + 1) * MOBA_BLOCK
        o_t = jnp.dot(vt_ref[h, :, 0:nk], p_s[h, slot, 0:nk, :],
                      preferred_element_type=F32)
        denom = o_t[MOBA_HEAD_DIM:MOBA_HEAD_DIM + 1, :]
        o_ref[blk(qb), hcols(h)] = (o_t[0:MOBA_HEAD_DIM, :] / denom).T.astype(BF16)

    heads = range(n_heads)
    m_next = [scores_stage(h, 0) for h in heads]
    for qb in range(nb):
        m_cur = m_next
        if qb + 1 < nb:
            m_next = [scores_stage(h, qb + 1) for h in heads]
        for h in heads:
            softmax_stage(h, qb, m_cur[h])
            value_stage(h, qb)


def _moba(proj3, k_mean, v_t, q_t):
    batch, seq, _ = proj3.shape
    nb = seq // MOBA_BLOCK
    hps = MOBA_HEADS_PER_STEP
    width = hps * MOBA_HEAD_DIM
    return pl.pallas_call(
        _moba_kernel,
        grid=(batch, MOBA_HEADS // hps),
        in_specs=[
            pl.BlockSpec((hps, MOBA_HEAD_DIM, seq), lambda b, g: (g, 0, b)),
            pl.BlockSpec((None, seq, width), lambda b, g: (b, 0, PROJ_OFF_KM // width + g)),
            pl.BlockSpec((hps, MOBA_VT_ROWS, seq), lambda b, g: (g, 0, b)),
            pl.BlockSpec((nb, width), lambda b, g: (b, g)),
        ],
        out_specs=pl.BlockSpec((None, seq, width), lambda b, g: (b, 0, g)),
        out_shape=jax.ShapeDtypeStruct((batch, seq, MOBA_W), BF16),
        scratch_shapes=[
            pltpu.VMEM((4 * nb, width), BF16),
            pltpu.VMEM((hps, 2, seq, MOBA_BLOCK), F32),
            pltpu.VMEM((hps, 2, seq, MOBA_BLOCK), BF16),
        ],
        compiler_params=pltpu.CompilerParams(
            dimension_semantics=("arbitrary", "arbitrary"), vmem_limit_bytes=V7X_VMEM_LIMIT_BYTES),
        name="block_attention",
    )(q_t, proj3, v_t, k_mean)


def _mix_ffn_kernel(x_ref, or_ref, om_ref, an_ref, wg_ref, bg_ref, wro_ref, wmo_ref, wo_ref,
                    fn_ref, wup_ref, cw_ref, cb_ref, wdn_ref, out_ref,
                    h_s, merged_s, act_s, ubuf_s, carry_s, *, tiles_per_seq):
    tm = x_ref.shape[0]
    halo = V7X_SUBLANES

    @pl.when(pl.program_id(0) % tiles_per_seq == 0)
    def _():
        carry_s[...] = jnp.zeros_like(carry_s)

    x = x_ref[...]
    h_s[...] = _rms(x, an_ref[...]).astype(BF16)
    for c in range(D_MODEL // MIX_COL_CHUNK):
        cs = slice(c * MIX_COL_CHUNK, (c + 1) * MIX_COL_CHUNK)
        cs2 = slice(D_MODEL + c * MIX_COL_CHUNK, D_MODEL + (c + 1) * MIX_COL_CHUNK)
        y_ret = jnp.dot(or_ref[...], wro_ref[:, cs], preferred_element_type=F32)
        y_moba = jnp.dot(om_ref[...], wmo_ref[:, cs], preferred_element_type=F32)
        g_ret = jax.nn.sigmoid(
            jnp.dot(h_s[...], wg_ref[:, cs], preferred_element_type=F32) + bg_ref[:, cs])
        g_moba = jax.nn.sigmoid(
            jnp.dot(h_s[...], wg_ref[:, cs2], preferred_element_type=F32) + bg_ref[:, cs2])
        merged_s[:, cs] = (g_ret * y_ret + g_moba * y_moba).astype(BF16)
    x1 = x + jnp.dot(merged_s[...], wo_ref[...], preferred_element_type=F32)
    out_ref[...] = x1
    h_s[...] = _rms(x1, fn_ref[...]).astype(BF16)

    def conv(u, slot, cols):
        ubuf_s[slot, 0:halo, :] = carry_s[:, cols]
        ubuf_s[slot, halo:halo + tm, :] = u
        carry_s[:, cols] = u[tm - halo:tm, :]
        u1 = ubuf_s[slot, halo - 1:halo - 1 + tm, :]
        u2 = ubuf_s[slot, halo - 2:halo - 2 + tm, :]
        return (cb_ref[:, cols] + cw_ref[0:1, cols] * u2 + cw_ref[1:2, cols] * u1
                + cw_ref[2:3, cols] * u)

    def down_proj(chunks):
        acc = None
        for k in chunks:
            d = jnp.dot(act_s[k], wdn_ref[k * FF_CHUNK:(k + 1) * FF_CHUNK, :],
                        preferred_element_type=F32)
            acc = d if acc is None else acc + d
        return acc

    n_ff = D_FF // FF_CHUNK
    for c in range(n_ff):
        gcols = slice(c * FF_CHUNK, (c + 1) * FF_CHUNK)
        vcols = slice(D_FF + c * FF_CHUNK, D_FF + (c + 1) * FF_CHUNK)
        d = jnp.dot(h_s[...], wup_ref[:, 2 * c * FF_CHUNK:2 * (c + 1) * FF_CHUNK],
                    preferred_element_type=F32)
        d_gate, d_val = d[:, 0:FF_CHUNK], d[:, FF_CHUNK:2 * FF_CHUNK]
        if c == n_ff - 1:
            out_ref[...] += down_proj(range(n_ff - 1))
        u_gate = conv(d_gate, 0, gcols)
        u_val = conv(d_val, 1, vcols)
        act_s[c] = (u_gate * jax.nn.sigmoid(u_gate) * u_val).astype(BF16)
    out_ref[...] += down_proj([n_ff - 1])


def _mix_ffn(x2, o_r, o_m, attn_norm, w_gate, b_gate, w_ret_o, w_moba_o, w_out,
             ffn_norm, w_up, conv_w, conv_b, w_down, seq):
    tokens = x2.shape[0]
    tile = lambda cols: pl.BlockSpec((TOKEN_TILE, cols), lambda i: (i, 0))
    return pl.pallas_call(
        functools.partial(_mix_ffn_kernel, tiles_per_seq=seq // TOKEN_TILE),
        grid=(tokens // TOKEN_TILE,),
        in_specs=[
            tile(D_MODEL), tile(RET_V), tile(MOBA_W),
            _resident((1, D_MODEL)), _resident((D_MODEL, 2 * D_MODEL)), _resident((1, 2 * D_MODEL)),
            _resident((RET_V, D_MODEL)), _resident((MOBA_W, D_MODEL)), _resident((D_MODEL, D_MODEL)),
            _resident((1, D_MODEL)), _resident((D_MODEL, UP_COLS)),
            _resident((CONV_WIDTH, 2 * D_FF)), _resident((1, 2 * D_FF)), _resident((D_FF, D_MODEL)),
        ],
        out_specs=tile(D_MODEL),
        out_shape=jax.ShapeDtypeStruct((tokens, D_MODEL), F32),
        scratch_shapes=[
            pltpu.VMEM((TOKEN_TILE, D_MODEL), BF16),
            pltpu.VMEM((TOKEN_TILE, D_MODEL), BF16),
            pltpu.VMEM((D_FF // FF_CHUNK, TOKEN_TILE, FF_CHUNK), BF16),
            pltpu.VMEM((2, TOKEN_TILE + V7X_SUBLANES, FF_CHUNK), F32),
            pltpu.VMEM((V7X_SUBLANES, 2 * D_FF), F32),
        ],
        compiler_params=pltpu.CompilerParams(
            dimension_semantics=("arbitrary",), vmem_limit_bytes=V7X_VMEM_LIMIT_BYTES),
        name="merge_ffn",
    )(x2, o_r, o_m, attn_norm, w_gate, b_gate, w_ret_o, w_moba_o, w_out,
      ffn_norm, w_up, conv_w, conv_b, w_down)


def _rotary_tables(seq):
    half = RET_QK_DIM // 2
    inv_freq = ROPE_BASE ** (-jnp.arange(half, dtype=F32) / half)
    ang = jnp.arange(seq).astype(F32)[:, None] * inv_freq[None, :]
    cos, sin = jnp.cos(ang), jnp.sin(ang)
    return jnp.concatenate([cos, cos], axis=-1), jnp.concatenate([-sin, sin], axis=-1)


def _retention_tables():
    log_gamma = jnp.log1p(-jnp.power(2.0, -5.0 - jnp.arange(RET_HEADS, dtype=F32)))
    pos = jnp.arange(RET_CHUNK, dtype=F32)
    diff = pos[:, None] - pos[None, :]
    decay = jnp.where(diff >= 0, jnp.exp(log_gamma[:, None, None] * jnp.maximum(diff, 0.0)), 0.0)
    zeta = jnp.exp(log_gamma[:, None] * (RET_CHUNK - 1 - pos)[None, :])
    xi = jnp.exp(log_gamma[:, None] * (pos + 1)[None, :])
    gamma_c = jnp.exp(log_gamma * RET_CHUNK)
    xi_b = jnp.broadcast_to(xi[:, :, None], (RET_HEADS, RET_CHUNK, RET_V_DIM))
    zeta_b = jnp.broadcast_to(zeta[:, :, None], (RET_HEADS, RET_CHUNK, RET_QK_DIM))
    gamma_b = jnp.broadcast_to(gamma_c[:, None, None], (RET_HEADS, 1, RET_V_DIM))
    return decay, xi_b, zeta_b, gamma_b


def _interleave_up_chunks(w_up):
    d_model = w_up.shape[0]
    w = w_up.reshape(d_model, 2, D_FF // FF_CHUNK, FF_CHUNK)
    w = w.transpose(0, 2, 1, 3).reshape(d_model, 2 * D_FF)
    return jnp.pad(w, ((0, 0), (0, UP_COLS - 2 * D_FF)))


def kernel(x, attn_norm, w_in, ret_norm_gain, ret_norm_bias, moba_q_gain, moba_k_gain,
           w_ret_o, w_moba_o, w_gate, b_gate, w_out, ffn_norm, w_up, conv_w, conv_b, w_down):
    batch, seq, d_model = x.shape
    depth = attn_norm.shape[0]
    assert d_model == D_MODEL and seq % TOKEN_TILE == 0 and seq % MOBA_BLOCK == 0
    cos, sin = _rotary_tables(seq)
    decay, xi_b, zeta_b, gamma_b = _retention_tables()
    x2 = x.reshape(batch * seq, d_model)
    for l in range(depth):
        q_gain = moba_q_gain[l][None] * (MOBA_HEAD_DIM ** -0.5 * LOG2_E)
        proj, k_mean, v_t, q_t = _in_projection(x2, attn_norm[l][None], cos, sin, q_gain,
                                                moba_k_gain[l][None], w_in[l].astype(BF16), seq)
        proj3 = proj.reshape(batch, seq, PROJ_W)
        o_r = _retention(proj3, decay, xi_b, zeta_b, gamma_b,
                         ret_norm_gain[l][None], ret_norm_bias[l][None])
        o_m = _moba(proj3, k_mean, v_t, q_t)
        x2 = _mix_ffn(
            x2, o_r.reshape(batch * seq, RET_V), o_m.reshape(batch * seq, MOBA_W),
            attn_norm[l][None], w_gate[l].astype(BF16), b_gate[l][None],
            w_ret_o[l].astype(BF16), w_moba_o[l].astype(BF16), w_out[l].astype(BF16),
            ffn_norm[l][None], _interleave_up_chunks(w_up[l]).astype(BF16), conv_w[l],
            conv_b[l][None],
            w_down[l].astype(BF16), seq)
    return x2.reshape(batch, seq, d_model)
```

```python
import functools

import jax
import jax.numpy as jnp
from jax import lax
from jax.experimental import pallas as pl
from jax.experimental.pallas import tpu as pltpu

F32 = jnp.float32
BF16 = jnp.bfloat16

D_MODEL = 1024
RET_HEADS = 4
RET_QK_DIM = 128
RET_V_DIM = 256
ROPE_BASE = 10000.0
MOBA_HEADS = 8
MOBA_HEAD_DIM = 128
MOBA_BLOCK = 256
MOBA_TOPK = 3
D_FF = 2816
CONV_WIDTH = 3
NORM_EPS = 1e-6
LOG2_E = 1.4426950408889634

RET_QK = RET_HEADS * RET_QK_DIM
RET_V = RET_HEADS * RET_V_DIM
MOBA_W = MOBA_HEADS * MOBA_HEAD_DIM
D_IN = 2 * RET_QK + 2 * RET_V + 3 * MOBA_W
OFF_QR, OFF_KR, OFF_VR, OFF_GR = 0, RET_QK, 2 * RET_QK, 2 * RET_QK + RET_V
OFF_QM = 2 * RET_QK + 2 * RET_V
OFF_KM, OFF_VM = OFF_QM + MOBA_W, OFF_QM + 2 * MOBA_W
PROJ_OFF_KM = OFF_QM
PROJ_W = PROJ_OFF_KM + MOBA_W

V7X_LANES = 128
V7X_SUBLANES = 8
V7X_VMEM_LIMIT_BYTES = 56 * 1024 * 1024

TOKEN_TILE = 512
PROJ_COL_CHUNK = 512
MIX_COL_CHUNK = 512
FF_CHUNK = 256
MOBA_HEADS_PER_STEP = 4
MOBA_VT_ROWS = MOBA_HEAD_DIM + 16
RET_HEADS_PER_STEP = 2
RET_CHUNK = 256


def _resident(shape):
    nd = len(shape)
    return pl.BlockSpec(shape, lambda *_: (0,) * nd, pipeline_mode=pl.Buffered(1))


def _rms(x, gain):
    inv = lax.rsqrt(jnp.mean(x * x, axis=-1, keepdims=True) + NORM_EPS)
    return (x * inv) * gain


def _inproj_kernel(x_ref, gain_ref, cos_ref, sin_ref, gq_ref, gk_ref, w_ref,
                   o_ref, km_ref, vt_ref, qt_ref, h_s):
    tm = x_ref.shape[0]
    h_s[...] = _rms(x_ref[...], gain_ref[...]).astype(BF16)
    cos = cos_ref[...]
    sin = sin_ref[...]
    half = RET_QK_DIM // 2
    for c in range(D_IN // PROJ_COL_CHUNK):
        c0 = c * PROJ_COL_CHUNK
        acc = jnp.dot(h_s[...], w_ref[:, c0:c0 + PROJ_COL_CHUNK], preferred_element_type=F32)
        if c0 < OFF_VR:
            k_scale = RET_QK_DIM ** -0.5 if c0 >= OFF_KR else None
            for s in range(PROJ_COL_CHUNK // RET_QK_DIM):
                seg = acc[:, s * RET_QK_DIM:(s + 1) * RET_QK_DIM]
                seg = seg * cos + pltpu.roll(seg, half, axis=1) * sin
                if k_scale is not None:
                    seg = seg * k_scale
                o_ref[:, c0 + s * RET_QK_DIM:c0 + (s + 1) * RET_QK_DIM] = seg.astype(BF16)
        elif OFF_QM <= c0 < OFF_VM:
            is_key = c0 >= OFF_KM
            gain = gk_ref[...] if is_key else gq_ref[...]
            for s in range(PROJ_COL_CHUNK // MOBA_HEAD_DIM):
                seg = _rms(acc[:, s * MOBA_HEAD_DIM:(s + 1) * MOBA_HEAD_DIM], gain)
                if is_key:
                    head = (c0 - OFF_KM) // MOBA_HEAD_DIM + s
                    kcols = slice(head * MOBA_HEAD_DIM, (head + 1) * MOBA_HEAD_DIM)
                    o_ref[:, PROJ_OFF_KM + kcols.start:PROJ_OFF_KM + kcols.stop] = seg.astype(BF16)
                    km_ref[:, kcols] = jnp.mean(
                        seg.reshape(tm // MOBA_BLOCK, MOBA_BLOCK, MOBA_HEAD_DIM), axis=1)
                else:
                    head = (c0 - OFF_QM) // MOBA_HEAD_DIM + s
                    qt_ref[head, :, :] = seg.T.astype(BF16)
        elif c0 >= OFF_VM:
            for s in range(PROJ_COL_CHUNK // MOBA_HEAD_DIM):
                head = (c0 - OFF_VM) // MOBA_HEAD_DIM + s
                seg = acc[:, s * MOBA_HEAD_DIM:(s + 1) * MOBA_HEAD_DIM]
                vt_ref[head, 0:MOBA_HEAD_DIM, :] = seg.T.astype(BF16)
                vt_ref[head, MOBA_HEAD_DIM:MOBA_VT_ROWS, :] = jnp.ones(
                    (MOBA_VT_ROWS - MOBA_HEAD_DIM, tm), BF16)
        elif c0 >= OFF_GR:
            o_ref[:, c0:c0 + PROJ_COL_CHUNK] = (acc * jax.nn.sigmoid(acc)).astype(BF16)
        else:
            o_ref[:, c0:c0 + PROJ_COL_CHUNK] = acc.astype(BF16)


def _in_projection(x2, attn_norm, cos, sin, q_gain, k_gain, w_in, seq):
    tokens = x2.shape[0]
    tiles_per_seq = seq // TOKEN_TILE
    blocks_per_tile = TOKEN_TILE // MOBA_BLOCK
    proj, k_mean, v_t, q_t = pl.pallas_call(
        _inproj_kernel,
        grid=(tokens // TOKEN_TILE,),
        in_specs=[
            pl.BlockSpec((TOKEN_TILE, D_MODEL), lambda i: (i, 0)),
            _resident((1, D_MODEL)),
            pl.BlockSpec((TOKEN_TILE, RET_QK_DIM), lambda i: (i % tiles_per_seq, 0)),
            pl.BlockSpec((TOKEN_TILE, RET_QK_DIM), lambda i: (i % tiles_per_seq, 0)),
            _resident((1, MOBA_HEAD_DIM)),
            _resident((1, MOBA_HEAD_DIM)),
            _resident((D_MODEL, D_IN)),
        ],
        out_specs=[
            pl.BlockSpec((TOKEN_TILE, PROJ_W), lambda i: (i, 0)),
            pl.BlockSpec((None, blocks_per_tile, MOBA_W), lambda i: (i, 0, 0)),
            pl.BlockSpec((MOBA_HEADS, MOBA_VT_ROWS, TOKEN_TILE), lambda i: (0, 0, i)),
            pl.BlockSpec((MOBA_HEADS, MOBA_HEAD_DIM, TOKEN_TILE), lambda i: (0, 0, i)),
        ],
        out_shape=[
            jax.ShapeDtypeStruct((tokens, PROJ_W), BF16),
            jax.ShapeDtypeStruct((tokens // TOKEN_TILE, blocks_per_tile, MOBA_W), F32),
            jax.ShapeDtypeStruct((MOBA_HEADS, MOBA_VT_ROWS, tokens), BF16),
            jax.ShapeDtypeStruct((MOBA_HEADS, MOBA_HEAD_DIM, tokens), BF16),
        ],
        scratch_shapes=[pltpu.VMEM((TOKEN_TILE, D_MODEL), BF16)],
        compiler_params=pltpu.CompilerParams(
            dimension_semantics=("arbitrary",), vmem_limit_bytes=V7X_VMEM_LIMIT_BYTES),
        name="in_projection",
    )(x2, attn_norm, cos, sin, q_gain, k_gain, w_in)
    return proj, k_mean.reshape(tokens // MOBA_BLOCK, MOBA_W), v_t, q_t


def _retention_kernel(q_ref, k_ref, v_ref, g_ref, dec_ref, xi_ref, zeta_ref, gc_ref,
                      gain_ref, bias_ref, o_ref):
    seq = q_ref.shape[0]
    n_heads = dec_ref.shape[0]
    state = [jnp.zeros((RET_QK_DIM, RET_V_DIM), F32) for _ in range(n_heads)]
    for c in range(seq // RET_CHUNK):
        rows = slice(c * RET_CHUNK, (c + 1) * RET_CHUNK)
        for h in range(n_heads):
            qk_cols = slice(h * RET_QK_DIM, (h + 1) * RET_QK_DIM)
            v_cols = slice(h * RET_V_DIM, (h + 1) * RET_V_DIM)
            qc, kc, vc = q_ref[rows, qk_cols], k_ref[rows, qk_cols], v_ref[rows, v_cols]
            scores = lax.dot_general(qc, kc, (((1,), (1,)), ((), ())),
                                     preferred_element_type=F32) * dec_ref[h]
            o = jnp.dot(scores.astype(BF16), vc, preferred_element_type=F32)
            o = o + jnp.dot(qc, state[h].astype(BF16), preferred_element_type=F32) * xi_ref[h]
            kz = (kc.astype(F32) * zeta_ref[h]).astype(BF16)
            state[h] = gc_ref[h] * state[h] + lax.dot_general(
                kz, vc, (((0,), (0,)), ((), ())), preferred_element_type=F32)
            mu = jnp.mean(o, axis=-1, keepdims=True)
            d = o - mu
            var = jnp.mean(d * d, axis=-1, keepdims=True)
            on = d * lax.rsqrt(var + NORM_EPS)
            silu_g = g_ref[rows, v_cols].astype(F32)
            o_ref[rows, v_cols] = ((on * gain_ref[:, v_cols] + bias_ref[:, v_cols])
                                   * silu_g).astype(BF16)


def _retention(proj3, decay, xi, zeta, gamma_c, gain, bias):
    batch, seq, _ = proj3.shape
    hps = RET_HEADS_PER_STEP
    qk_w, v_w = hps * RET_QK_DIM, hps * RET_V_DIM
    qk_blk = lambda off: pl.BlockSpec((None, seq, qk_w), lambda b, g: (b, 0, off // qk_w + g))
    v_blk = lambda off: pl.BlockSpec((None, seq, v_w), lambda b, g: (b, 0, off // v_w + g))
    per_head = lambda r, c: pl.BlockSpec((hps, r, c), lambda b, g: (g, 0, 0))
    return pl.pallas_call(
        _retention_kernel,
        grid=(batch, RET_HEADS // hps),
        in_specs=[
            qk_blk(OFF_QR), qk_blk(OFF_KR), v_blk(OFF_VR), v_blk(OFF_GR),
            per_head(RET_CHUNK, RET_CHUNK), per_head(RET_CHUNK, RET_V_DIM),
            per_head(RET_CHUNK, RET_QK_DIM), per_head(1, RET_V_DIM),
            pl.BlockSpec((1, v_w), lambda b, g: (0, g)),
            pl.BlockSpec((1, v_w), lambda b, g: (0, g)),
        ],
        out_specs=pl.BlockSpec((None, seq, v_w), lambda b, g: (b, 0, g)),
        out_shape=jax.ShapeDtypeStruct((batch, seq, RET_V), BF16),
        compiler_params=pltpu.CompilerParams(
            dimension_semantics=("arbitrary", "arbitrary"), vmem_limit_bytes=V7X_VMEM_LIMIT_BYTES),
        name="retention",
    )(proj3, proj3, proj3, proj3, decay, xi, zeta, gamma_c, gain, bias)


def _moba_kernel(qt_ref, k_ref, vt_ref, km_ref, o_ref, km3_s, s_s, p_s):
    seq = k_ref.shape[0]
    n_heads = vt_ref.shape[0]
    nb = seq // MOBA_BLOCK
    km = km_ref[...]
    km_hi = km.astype(BF16)
    km_mid = (km - km_hi.astype(F32)).astype(BF16)
    km_lo = (km - km_hi.astype(F32) - km_mid.astype(F32)).astype(BF16)
    km3_s[...] = jnp.concatenate([km_hi, km_mid, km_lo, jnp.zeros_like(km_hi)], axis=0)

    key_i = lax.broadcasted_iota(jnp.int32, (MOBA_BLOCK, MOBA_BLOCK), 0)
    qry_i = lax.broadcasted_iota(jnp.int32, (MOBA_BLOCK, MOBA_BLOCK), 1)
    causal = key_i <= qry_i

    def blk(j):
        return slice(j * MOBA_BLOCK, (j + 1) * MOBA_BLOCK)

    def hcols(h):
        return slice(h * MOBA_HEAD_DIM, (h + 1) * MOBA_HEAD_DIM)

    def keep_masks(h, qb):
        if qb <= MOBA_TOPK:
            return [None] * qb
        parts = jnp.dot(km3_s[:, hcols(h)], qt_ref[h, :, blk(qb)],
                        preferred_element_type=F32)
        gate = parts[0:nb] + (parts[nb:2 * nb] + parts[2 * nb:3 * nb])
        g = [gate[i:i + 1, :] for i in range(qb)]
        keep = []
        for j in range(qb):
            beaten = jnp.zeros((1, MOBA_BLOCK), jnp.int32)
            for i in range(qb):
                if i < j:
                    beaten = beaten + (g[i] >= g[j]).astype(jnp.int32)
                elif i > j:
                    beaten = beaten + (g[i] > g[j]).astype(jnp.int32)
            keep.append(beaten < MOBA_TOPK)
        return keep

    def scores_stage(h, qb):
        slot = qb % 2
        keep = keep_masks(h, qb)
        q_blk_t = qt_ref[h, :, blk(qb)]
        m = None
        for j in range(qb + 1):
            s = jnp.dot(k_ref[blk(j), hcols(h)], q_blk_t,
                        preferred_element_type=F32)
            if j == qb:
                s = jnp.where(causal, s, -jnp.inf)
            elif keep[j] is not None:
                s = jnp.where(keep[j], s, -jnp.inf)
            s_s[h, slot, blk(j), :] = s
            mj = s_s[h, slot, blk(j), :].max(axis=0, keepdims=True)
            m = mj if m is None else jnp.maximum(m, mj)
        return m

    def softmax_stage(h, qb, m):
        slot = qb % 2
        for j in range(qb + 1):
            p_s[h, slot, blk(j), :] = jnp.exp2(s_s[h, slot, blk(j), :] - m).astype(BF16)

    def value_stage(h, qb):
        slot = qb % 2
        nk = (qb + 1) * MOBA_BLOCK
        o_t = jnp.dot(vt_ref[h, :, 0:nk], p_s[h, slot, 0:nk, :],
                      preferred_element_type=F32)
        denom = o_t[MOBA_HEAD_DIM:MOBA_HEAD_DIM + 1, :]
        o_ref[blk(qb), hcols(h)] = (o_t[0:MOBA_HEAD_DIM, :] / denom).T.astype(BF16)

    heads = range(n_heads)
    m_next = [scores_stage(h, 0) for h in heads]
    for qb in range(nb):
        m_cur = m_next
        if qb + 1 < nb:
            m_next = [scores_stage(h, qb + 1) for h in heads]
        for h in heads:
            softmax_stage(h, qb, m_cur[h])
            value_stage(h, qb)


def _moba(proj3, k_mean, v_t, q_t):
    batch, seq, _ = proj3.shape
    nb = seq // MOBA_BLOCK
    hps = MOBA_HEADS_PER_STEP
    width = hps * MOBA_HEAD_DIM
    return pl.pallas_call(
        _moba_kernel,
        grid=(batch, MOBA_HEADS // hps),
        in_specs=[
            pl.BlockSpec((hps, MOBA_HEAD_DIM, seq), lambda b, g: (g, 0, b)),
            pl.BlockSpec((None, seq, width), lambda b, g: (b, 0, PROJ_OFF_KM // width + g)),
            pl.BlockSpec((hps, MOBA_VT_ROWS, seq), lambda b, g: (g, 0, b)),
            pl.BlockSpec((nb, width), lambda b, g: (b, g)),
        ],
        out_specs=pl.BlockSpec((None, seq, width), lambda b, g: (b, 0, g)),
        out_shape=jax.ShapeDtypeStruct((batch, seq, MOBA_W), BF16),
        scratch_shapes=[
            pltpu.VMEM((4 * nb, width), BF16),
            pltpu.VMEM((hps, 2, seq, MOBA_BLOCK), F32),
            pltpu.VMEM((hps, 2, seq, MOBA_BLOCK), BF16),
        ],
        compiler_params=pltpu.CompilerParams(
            dimension_semantics=("arbitrary", "arbitrary"), vmem_limit_bytes=V7X_VMEM_LIMIT_BYTES),
        name="block_attention",
    )(q_t, proj3, v_t, k_mean)


def _mix_ffn_kernel(x_ref, or_ref, om_ref, an_ref, wg_ref, bg_ref, wro_ref, wmo_ref, wo_ref,
                    fn_ref, wup_ref, cw_ref, cb_ref, wdn_ref, out_ref,
                    h_s, merged_s, act_s, ubuf_s, carry_s, *, tiles_per_seq):
    tm = x_ref.shape[0]
    halo = V7X_SUBLANES

    @pl.when(pl.program_id(0) % tiles_per_seq == 0)
    def _():
        carry_s[...] = jnp.zeros_like(carry_s)

    x = x_ref[...]
    h_s[...] = _rms(x, an_ref[...]).astype(BF16)
    for c in range(D_MODEL // MIX_COL_CHUNK):
        cs = slice(c * MIX_COL_CHUNK, (c + 1) * MIX_COL_CHUNK)
        cs2 = slice(D_MODEL + c * MIX_COL_CHUNK, D_MODEL + (c + 1) * MIX_COL_CHUNK)
        y_ret = jnp.dot(or_ref[...], wro_ref[:, cs], preferred_element_type=F32)
        y_moba = jnp.dot(om_ref[...], wmo_ref[:, cs], preferred_element_type=F32)
        g_ret = jax.nn.sigmoid(
            jnp.dot(h_s[...], wg_ref[:, cs], preferred_element_type=F32) + bg_ref[:, cs])
        g_moba = jax.nn.sigmoid(
            jnp.dot(h_s[...], wg_ref[:, cs2], preferred_element_type=F32) + bg_ref[:, cs2])
        merged_s[:, cs] = (g_ret * y_ret + g_moba * y_moba).astype(BF16)
    x1 = x + jnp.dot(merged_s[...], wo_ref[...], preferred_element_type=F32)
    out_ref[...] = x1
    h_s[...] = _rms(x1, fn_ref[...]).astype(BF16)

    def conv(u, slot, cols):
        ubuf_s[slot, 0:halo, :] = carry_s[:, cols]
        ubuf_s[slot, halo:halo + tm, :] = u
        carry_s[:, cols] = u[tm - halo:tm, :]
        u1 = ubuf_s[slot, halo - 1:halo - 1 + tm, :]
        u2 = ubuf_s[slot, halo - 2:halo - 2 + tm, :]
        return (cb_ref[:, cols] + cw_ref[0:1, cols] * u2 + cw_ref[1:2, cols] * u1
                + cw_ref[2:3, cols] * u)

    n_ff = D_FF // FF_CHUNK
    head = (n_ff - 1) * FF_CHUNK
    for c in range(n_ff):
        gcols = slice(c * FF_CHUNK, (c + 1) * FF_CHUNK)
        vcols = slice(D_FF + c * FF_CHUNK, D_FF + (c + 1) * FF_CHUNK)
        d_gate = jnp.dot(h_s[...], wup_ref[:, gcols], preferred_element_type=F32)
        d_val = jnp.dot(h_s[...], wup_ref[:, vcols], preferred_element_type=F32)
        if c == n_ff - 1:
            out_ref[...] += jnp.dot(act_s[:, 0:head], wdn_ref[0:head, :],
                                    preferred_element_type=F32)
        u_gate = conv(d_gate, 0, gcols)
        u_val = conv(d_val, 1, vcols)
        act_s[:, gcols] = (u_gate * jax.nn.sigmoid(u_gate) * u_val).astype(BF16)
    out_ref[...] += jnp.dot(act_s[:, head:D_FF], wdn_ref[head:D_FF, :], preferred_element_type=F32)


def _mix_ffn(x2, o_r, o_m, attn_norm, w_gate, b_gate, w_ret_o, w_moba_o, w_out,
             ffn_norm, w_up, conv_w, conv_b, w_down, seq):
    tokens = x2.shape[0]
    tile = lambda cols: pl.BlockSpec((TOKEN_TILE, cols), lambda i: (i, 0))
    return pl.pallas_call(
        functools.partial(_mix_ffn_kernel, tiles_per_seq=seq // TOKEN_TILE),
        grid=(tokens // TOKEN_TILE,),
        in_specs=[
            tile(D_MODEL), tile(RET_V), tile(MOBA_W),
            _resident((1, D_MODEL)), _resident((D_MODEL, 2 * D_MODEL)), _resident((1, 2 * D_MODEL)),
            _resident((RET_V, D_MODEL)), _resident((MOBA_W, D_MODEL)), _resident((D_MODEL, D_MODEL)),
            _resident((1, D_MODEL)), _resident((D_MODEL, 2 * D_FF)),
            _resident((CONV_WIDTH, 2 * D_FF)), _resident((1, 2 * D_FF)), _resident((D_FF, D_MODEL)),
        ],
        out_specs=tile(D_MODEL),
        out_shape=jax.ShapeDtypeStruct((tokens, D_MODEL), F32),
        scratch_shapes=[
            pltpu.VMEM((TOKEN_TILE, D_MODEL), BF16),
            pltpu.VMEM((TOKEN_TILE, D_MODEL), BF16),
            pltpu.VMEM((TOKEN_TILE, D_FF), BF16),
            pltpu.VMEM((2, TOKEN_TILE + V7X_SUBLANES, FF_CHUNK), F32),
            pltpu.VMEM((V7X_SUBLANES, 2 * D_FF), F32),
        ],
        compiler_params=pltpu.CompilerParams(
            dimension_semantics=("arbitrary",), vmem_limit_bytes=V7X_VMEM_LIMIT_BYTES),
        name="merge_ffn",
    )(x2, o_r, o_m, attn_norm, w_gate, b_gate, w_ret_o, w_moba_o, w_out,
      ffn_norm, w_up, conv_w, conv_b, w_down)


def _rotary_tables(seq):
    half = RET_QK_DIM // 2
    inv_freq = ROPE_BASE ** (-jnp.arange(half, dtype=F32) / half)
    ang = jnp.arange(seq).astype(F32)[:, None] * inv_freq[None, :]
    cos, sin = jnp.cos(ang), jnp.sin(ang)
    return jnp.concatenate([cos, cos], axis=-1), jnp.concatenate([-sin, sin], axis=-1)


def _retention_tables():
    log_gamma = jnp.log1p(-jnp.power(2.0, -5.0 - jnp.arange(RET_HEADS, dtype=F32)))
    pos = jnp.arange(RET_CHUNK, dtype=F32)
    diff = pos[:, None] - pos[None, :]
    decay = jnp.where(diff >= 0, jnp.exp(log_gamma[:, None, None] * jnp.maximum(diff, 0.0)), 0.0)
    zeta = jnp.exp(log_gamma[:, None] * (RET_CHUNK - 1 - pos)[None, :])
    xi = jnp.exp(log_gamma[:, None] * (pos + 1)[None, :])
    gamma_c = jnp.exp(log_gamma * RET_CHUNK)
    xi_b = jnp.broadcast_to(xi[:, :, None], (RET_HEADS, RET_CHUNK, RET_V_DIM))
    zeta_b = jnp.broadcast_to(zeta[:, :, None], (RET_HEADS, RET_CHUNK, RET_QK_DIM))
    gamma_b = jnp.broadcast_to(gamma_c[:, None, None], (RET_HEADS, 1, RET_V_DIM))
    return decay, xi_b, zeta_b, gamma_b


def kernel(x, attn_norm, w_in, ret_norm_gain, ret_norm_bias, moba_q_gain, moba_k_gain,
           w_ret_o, w_moba_o, w_gate, b_gate, w_out, ffn_norm, w_up, conv_w, conv_b, w_down):
    batch, seq, d_model = x.shape
    depth = attn_norm.shape[0]
    assert d_model == D_MODEL and seq % TOKEN_TILE == 0 and seq % MOBA_BLOCK == 0
    cos, sin = _rotary_tables(seq)
    decay, xi_b, zeta_b, gamma_b = _retention_tables()
    x2 = x.reshape(batch * seq, d_model)
    for l in range(depth):
        q_gain = moba_q_gain[l][None] * (MOBA_HEAD_DIM ** -0.5 * LOG2_E)
        proj, k_mean, v_t, q_t = _in_projection(x2, attn_norm[l][None], cos, sin, q_gain,
                                                moba_k_gain[l][None], w_in[l].astype(BF16), seq)
        proj3 = proj.reshape(batch, seq, PROJ_W)
        o_r = _retention(proj3, decay, xi_b, zeta_b, gamma_b,
                         ret_norm_gain[l][None], ret_norm_bias[l][None])
        o_m = _moba(proj3, k_mean, v_t, q_t)
        x2 = _mix_ffn(
            x2, o_r.reshape(batch * seq, RET_V), o_m.reshape(batch * seq, MOBA_W),
            attn_norm[l][None], w_gate[l].astype(BF16), b_gate[l][None],
            w_ret_o[l].astype(BF16), w_moba_o[l].astype(BF16), w_out[l].astype(BF16),
            ffn_norm[l][None], w_up[l].astype(BF16), conv_w[l], conv_b[l][None],
            w_down[l].astype(BF16), seq)
    return x2.reshape(batch, seq, d_model)
```

```python
import functools

import jax
import jax.numpy as jnp
from jax import lax
from jax.experimental import pallas as pl
from jax.experimental.pallas import tpu as pltpu

F32 = jnp.float32
BF16 = jnp.bfloat16

D_MODEL = 1024
RET_HEADS = 4
RET_QK_DIM = 128
RET_V_DIM = 256
ROPE_BASE = 10000.0
MOBA_HEADS = 8
MOBA_HEAD_DIM = 128
MOBA_BLOCK = 256
MOBA_TOPK = 3
D_FF = 2816
CONV_WIDTH = 3
NORM_EPS = 1e-6
LOG2_E = 1.4426950408889634

RET_QK = RET_HEADS * RET_QK_DIM
RET_V = RET_HEADS * RET_V_DIM
MOBA_W = MOBA_HEADS * MOBA_HEAD_DIM
D_IN = 2 * RET_QK + 2 * RET_V + 3 * MOBA_W
OFF_QR, OFF_KR, OFF_VR, OFF_GR = 0, RET_QK, 2 * RET_QK, 2 * RET_QK + RET_V
OFF_QM = 2 * RET_QK + 2 * RET_V
OFF_KM, OFF_VM = OFF_QM + MOBA_W, OFF_QM + 2 * MOBA_W
PROJ_OFF_KM = OFF_QM
PROJ_W = PROJ_OFF_KM + MOBA_W

V7X_LANES = 128
V7X_SUBLANES = 8
V7X_VMEM_LIMIT_BYTES = 56 * 1024 * 1024

TOKEN_TILE = 512
PROJ_COL_CHUNK = 512
MIX_COL_CHUNK = 512
FF_CHUNK = 256
MOBA_HEADS_PER_STEP = 4
MOBA_VT_ROWS = MOBA_HEAD_DIM + 16
RET_HEADS_PER_STEP = 2
RET_CHUNK = 256


def _resident(shape):
    nd = len(shape)
    return pl.BlockSpec(shape, lambda *_: (0,) * nd, pipeline_mode=pl.Buffered(1))


def _rms(x, gain):
    inv = lax.rsqrt(jnp.mean(x * x, axis=-1, keepdims=True) + NORM_EPS)
    return (x * inv) * gain


def _inproj_kernel(x_ref, gain_ref, cos_ref, sin_ref, gq_ref, gk_ref, w_ref,
                   o_ref, km_ref, vt_ref, qt_ref, h_s):
    tm = x_ref.shape[0]
    h_s[...] = _rms(x_ref[...], gain_ref[...]).astype(BF16)
    cos = cos_ref[...]
    sin = sin_ref[...]
    half = RET_QK_DIM // 2
    starts = list(range(0, D_IN, PROJ_COL_CHUNK))
    order = ([c0 for c0 in starts if c0 >= OFF_QM][::-1]
             + [c0 for c0 in starts if c0 < OFF_VR]
             + [c0 for c0 in starts if OFF_GR <= c0 < OFF_QM]
             + [c0 for c0 in starts if OFF_VR <= c0 < OFF_GR])
    for c0 in order:
        acc = jnp.dot(h_s[...], w_ref[:, c0:c0 + PROJ_COL_CHUNK], preferred_element_type=F32)
        if c0 < OFF_VR:
            k_scale = RET_QK_DIM ** -0.5 if c0 >= OFF_KR else None
            for s in range(PROJ_COL_CHUNK // RET_QK_DIM):
                seg = acc[:, s * RET_QK_DIM:(s + 1) * RET_QK_DIM]
                seg = seg * cos + pltpu.roll(seg, half, axis=1) * sin
                if k_scale is not None:
                    seg = seg * k_scale
                o_ref[:, c0 + s * RET_QK_DIM:c0 + (s + 1) * RET_QK_DIM] = seg.astype(BF16)
        elif OFF_QM <= c0 < OFF_VM:
            is_key = c0 >= OFF_KM
            gain = gk_ref[...] if is_key else gq_ref[...]
            for s in range(PROJ_COL_CHUNK // MOBA_HEAD_DIM):
                seg = _rms(acc[:, s * MOBA_HEAD_DIM:(s + 1) * MOBA_HEAD_DIM], gain)
                if is_key:
                    head = (c0 - OFF_KM) // MOBA_HEAD_DIM + s
                    kcols = slice(head * MOBA_HEAD_DIM, (head + 1) * MOBA_HEAD_DIM)
                    o_ref[:, PROJ_OFF_KM + kcols.start:PROJ_OFF_KM + kcols.stop] = seg.astype(BF16)
                    km_ref[:, kcols] = jnp.mean(
                        seg.reshape(tm // MOBA_BLOCK, MOBA_BLOCK, MOBA_HEAD_DIM), axis=1)
                else:
                    head = (c0 - OFF_QM) // MOBA_HEAD_DIM + s
                    qt_ref[head, :, :] = seg.T.astype(BF16)
        elif c0 >= OFF_VM:
            for s in range(PROJ_COL_CHUNK // MOBA_HEAD_DIM):
                head = (c0 - OFF_VM) // MOBA_HEAD_DIM + s
                seg = acc[:, s * MOBA_HEAD_DIM:(s + 1) * MOBA_HEAD_DIM]
                vt_ref[head, 0:MOBA_HEAD_DIM, :] = seg.T.astype(BF16)
                vt_ref[head, MOBA_HEAD_DIM:MOBA_VT_ROWS, :] = jnp.ones(
                    (MOBA_VT_ROWS - MOBA_HEAD_DIM, tm), BF16)
        else:
            o_ref[:, c0:c0 + PROJ_COL_CHUNK] = acc.astype(BF16)


def _in_projection(x2, attn_norm, cos, sin, q_gain, k_gain, w_in, seq):
    tokens = x2.shape[0]
    tiles_per_seq = seq // TOKEN_TILE
    blocks_per_tile = TOKEN_TILE // MOBA_BLOCK
    proj, k_mean, v_t, q_t = pl.pallas_call(
        _inproj_kernel,
        grid=(tokens // TOKEN_TILE,),
        in_specs=[
            pl.BlockSpec((TOKEN_TILE, D_MODEL), lambda i: (i, 0)),
            _resident((1, D_MODEL)),
            pl.BlockSpec((TOKEN_TILE, RET_QK_DIM), lambda i: (i % tiles_per_seq, 0)),
            pl.BlockSpec((TOKEN_TILE, RET_QK_DIM), lambda i: (i % tiles_per_seq, 0)),
            _resident((1, MOBA_HEAD_DIM)),
            _resident((1, MOBA_HEAD_DIM)),
            _resident((D_MODEL, D_IN)),
        ],
        out_specs=[
            pl.BlockSpec((TOKEN_TILE, PROJ_W), lambda i: (i, 0)),
            pl.BlockSpec((None, blocks_per_tile, MOBA_W), lambda i: (i, 0, 0)),
            pl.BlockSpec((MOBA_HEADS, MOBA_VT_ROWS, TOKEN_TILE), lambda i: (0, 0, i)),
            pl.BlockSpec((MOBA_HEADS, MOBA_HEAD_DIM, TOKEN_TILE), lambda i: (0, 0, i)),
        ],
        out_shape=[
            jax.ShapeDtypeStruct((tokens, PROJ_W), BF16),
            jax.ShapeDtypeStruct((tokens // TOKEN_TILE, blocks_per_tile, MOBA_W), F32),
            jax.ShapeDtypeStruct((MOBA_HEADS, MOBA_VT_ROWS, tokens), BF16),
            jax.ShapeDtypeStruct((MOBA_HEADS, MOBA_HEAD_DIM, tokens), BF16),
        ],
        scratch_shapes=[pltpu.VMEM((TOKEN_TILE, D_MODEL), BF16)],
        compiler_params=pltpu.CompilerParams(
            dimension_semantics=("arbitrary",), vmem_limit_bytes=V7X_VMEM_LIMIT_BYTES),
        name="in_projection",
    )(x2, attn_norm, cos, sin, q_gain, k_gain, w_in)
    return proj, k_mean.reshape(tokens // MOBA_BLOCK, MOBA_W), v_t, q_t


def _retention_kernel(q_ref, k_ref, v_ref, g_ref, dec_ref, xi_ref, zeta_ref, gc_ref,
                      gain_ref, bias_ref, o_ref):
    seq = q_ref.shape[0]
    n_heads = dec_ref.shape[0]
    state = [jnp.zeros((RET_QK_DIM, RET_V_DIM), F32) for _ in range(n_heads)]
    for c in range(seq // RET_CHUNK):
        rows = slice(c * RET_CHUNK, (c + 1) * RET_CHUNK)
        for h in range(n_heads):
            qk_cols = slice(h * RET_QK_DIM, (h + 1) * RET_QK_DIM)
            v_cols = slice(h * RET_V_DIM, (h + 1) * RET_V_DIM)
            qc, kc, vc = q_ref[rows, qk_cols], k_ref[rows, qk_cols], v_ref[rows, v_cols]
            scores = lax.dot_general(qc, kc, (((1,), (1,)), ((), ())),
                                     preferred_element_type=F32) * dec_ref[h]
            o = jnp.dot(scores.astype(BF16), vc, preferred_element_type=F32)
            o = o + jnp.dot(qc, state[h].astype(BF16), preferred_element_type=F32) * xi_ref[h]
            kz = (kc.astype(F32) * zeta_ref[h]).astype(BF16)
            state[h] = gc_ref[h] * state[h] + lax.dot_general(
                kz, vc, (((0,), (0,)), ((), ())), preferred_element_type=F32)
            mu = jnp.mean(o, axis=-1, keepdims=True)
            d = o - mu
            var = jnp.mean(d * d, axis=-1, keepdims=True)
            on = d * lax.rsqrt(var + NORM_EPS)
            g = g_ref[rows, v_cols].astype(F32)
            o_ref[rows, v_cols] = ((on * gain_ref[:, v_cols] + bias_ref[:, v_cols])
                                   * (g * jax.nn.sigmoid(g))).astype(BF16)


def _retention(proj3, decay, xi, zeta, gamma_c, gain, bias):
    batch, seq, _ = proj3.shape
    hps = RET_HEADS_PER_STEP
    qk_w, v_w = hps * RET_QK_DIM, hps * RET_V_DIM
    qk_blk = lambda off: pl.BlockSpec((None, seq, qk_w), lambda b, g: (b, 0, off // qk_w + g))
    v_blk = lambda off: pl.BlockSpec((None, seq, v_w), lambda b, g: (b, 0, off // v_w + g))
    per_head = lambda r, c: pl.BlockSpec((hps, r, c), lambda b, g: (g, 0, 0))
    return pl.pallas_call(
        _retention_kernel,
        grid=(batch, RET_HEADS // hps),
        in_specs=[
            qk_blk(OFF_QR), qk_blk(OFF_KR), v_blk(OFF_VR), v_blk(OFF_GR),
            per_head(RET_CHUNK, RET_CHUNK), per_head(RET_CHUNK, RET_V_DIM),
            per_head(RET_CHUNK, RET_QK_DIM), per_head(1, RET_V_DIM),
            pl.BlockSpec((1, v_w), lambda b, g: (0, g)),
            pl.BlockSpec((1, v_w), lambda b, g: (0, g)),
        ],
        out_specs=pl.BlockSpec((None, seq, v_w), lambda b, g: (b, 0, g)),
        out_shape=jax.ShapeDtypeStruct((batch, seq, RET_V), BF16),
        compiler_params=pltpu.CompilerParams(
            dimension_semantics=("arbitrary", "arbitrary"), vmem_limit_bytes=V7X_VMEM_LIMIT_BYTES),
        name="retention",
    )(proj3, proj3, proj3, proj3, decay, xi, zeta, gamma_c, gain, bias)


def _moba_kernel(qt_ref, k_ref, vt_ref, km_ref, o_ref, km3_s, s_s, p_s):
    seq = k_ref.shape[0]
    n_heads = vt_ref.shape[0]
    nb = seq // MOBA_BLOCK
    km = km_ref[...]
    km_hi = km.astype(BF16)
    km_mid = (km - km_hi.astype(F32)).astype(BF16)
    km_lo = (km - km_hi.astype(F32) - km_mid.astype(F32)).astype(BF16)
    km3_s[...] = jnp.concatenate([km_hi, km_mid, km_lo, jnp.zeros_like(km_hi)], axis=0)

    key_i = lax.broadcasted_iota(jnp.int32, (MOBA_BLOCK, MOBA_BLOCK), 0)
    qry_i = lax.broadcasted_iota(jnp.int32, (MOBA_BLOCK, MOBA_BLOCK), 1)
    causal = key_i <= qry_i

    def blk(j):
        return slice(j * MOBA_BLOCK, (j + 1) * MOBA_BLOCK)

    def hcols(h):
        return slice(h * MOBA_HEAD_DIM, (h + 1) * MOBA_HEAD_DIM)

    def keep_masks(h, qb):
        if qb <= MOBA_TOPK:
            return [None] * qb
        parts = jnp.dot(km3_s[:, hcols(h)], qt_ref[h, :, blk(qb)],
                        preferred_element_type=F32)
        gate = parts[0:nb] + (parts[nb:2 * nb] + parts[2 * nb:3 * nb])
        g = [gate[i:i + 1, :] for i in range(qb)]
        keep = []
        for j in range(qb):
            beaten = jnp.zeros((1, MOBA_BLOCK), jnp.int32)
            for i in range(qb):
                if i < j:
                    beaten = beaten + (g[i] >= g[j]).astype(jnp.int32)
                elif i > j:
                    beaten = beaten + (g[i] > g[j]).astype(jnp.int32)
            keep.append(beaten < MOBA_TOPK)
        return keep

    def scores_stage(h, qb):
        slot = qb % 2
        keep = keep_masks(h, qb)
        q_blk_t = qt_ref[h, :, blk(qb)]
        m = None
        for j in range(qb + 1):
            s = jnp.dot(k_ref[blk(j), hcols(h)], q_blk_t,
                        preferred_element_type=F32)
            if j == qb:
                s = jnp.where(causal, s, -jnp.inf)
            elif keep[j] is not None:
                s = jnp.where(keep[j], s, -jnp.inf)
            s_s[h, slot, blk(j), :] = s
            mj = s_s[h, slot, blk(j), :].max(axis=0, keepdims=True)
            m = mj if m is None else jnp.maximum(m, mj)
        return m

    def softmax_stage(h, qb, m):
        slot = qb % 2
        for j in range(qb + 1):
            p_s[h, slot, blk(j), :] = jnp.exp2(s_s[h, slot, blk(j), :] - m).astype(BF16)

    def value_stage(h, qb):
        slot = qb % 2
        nk = (qb + 1) * MOBA_BLOCK
        o_t = jnp.dot(vt_ref[h, :, 0:nk], p_s[h, slot, 0:nk, :],
                      preferred_element_type=F32)
        denom = o_t[MOBA_HEAD_DIM:MOBA_HEAD_DIM + 1, :]
        o_ref[blk(qb), hcols(h)] = (o_t[0:MOBA_HEAD_DIM, :] / denom).T.astype(BF16)

    heads = range(n_heads)
    m_next = [scores_stage(h, 0) for h in heads]
    for qb in range(nb):
        m_cur = m_next
        if qb + 1 < nb:
            m_next = [scores_stage(h, qb + 1) for h in heads]
        for h in heads:
            softmax_stage(h, qb, m_cur[h])
            value_stage(h, qb)


def _moba(proj3, k_mean, v_t, q_t):
    batch, seq, _ = proj3.shape
    nb = seq // MOBA_BLOCK
    hps = MOBA_HEADS_PER_STEP
    width = hps * MOBA_HEAD_DIM
    return pl.pallas_call(
        _moba_kernel,
        grid=(batch, MOBA_HEADS // hps),
        in_specs=[
            pl.BlockSpec((hps, MOBA_HEAD_DIM, seq), lambda b, g: (g, 0, b)),
            pl.BlockSpec((None, seq, width), lambda b, g: (b, 0, PROJ_OFF_KM // width + g)),
            pl.BlockSpec((hps, MOBA_VT_ROWS, seq), lambda b, g: (g, 0, b)),
            pl.BlockSpec((nb, width), lambda b, g: (b, g)),
        ],
        out_specs=pl.BlockSpec((None, seq, width), lambda b, g: (b, 0, g)),
        out_shape=jax.ShapeDtypeStruct((batch, seq, MOBA_W), BF16),
        scratch_shapes=[
            pltpu.VMEM((4 * nb, width), BF16),
            pltpu.VMEM((hps, 2, seq, MOBA_BLOCK), F32),
            pltpu.VMEM((hps, 2, seq, MOBA_BLOCK), BF16),
        ],
        compiler_params=pltpu.CompilerParams(
            dimension_semantics=("arbitrary", "arbitrary"), vmem_limit_bytes=V7X_VMEM_LIMIT_BYTES),
        name="block_attention",
    )(q_t, proj3, v_t, k_mean)


def _mix_ffn_kernel(x_ref, or_ref, om_ref, an_ref, wg_ref, bg_ref, wro_ref, wmo_ref, wo_ref,
                    fn_ref, wup_ref, cw_ref, cb_ref, wdn_ref, out_ref,
                    h_s, merged_s, act_s, ubuf_s, carry_s, *, tiles_per_seq):
    tm = x_ref.shape[0]
    halo = V7X_SUBLANES

    @pl.when(pl.program_id(0) % tiles_per_seq == 0)
    def _():
        carry_s[...] = jnp.zeros_like(carry_s)

    x = x_ref[...]
    h_s[...] = _rms(x, an_ref[...]).astype(BF16)
    for c in range(D_MODEL // MIX_COL_CHUNK):
        cs = slice(c * MIX_COL_CHUNK, (c + 1) * MIX_COL_CHUNK)
        cs2 = slice(D_MODEL + c * MIX_COL_CHUNK, D_MODEL + (c + 1) * MIX_COL_CHUNK)
        y_ret = jnp.dot(or_ref[...], wro_ref[:, cs], preferred_element_type=F32)
        y_moba = jnp.dot(om_ref[...], wmo_ref[:, cs], preferred_element_type=F32)
        g_ret = jax.nn.sigmoid(
            jnp.dot(h_s[...], wg_ref[:, cs], preferred_element_type=F32) + bg_ref[:, cs])
        g_moba = jax.nn.sigmoid(
            jnp.dot(h_s[...], wg_ref[:, cs2], preferred_element_type=F32) + bg_ref[:, cs2])
        merged_s[:, cs] = (g_ret * y_ret + g_moba * y_moba).astype(BF16)
    x1 = x + jnp.dot(merged_s[...], wo_ref[...], preferred_element_type=F32)
    out_ref[...] = x1
    h_s[...] = _rms(x1, fn_ref[...]).astype(BF16)

    def conv(u, slot, cols):
        ubuf_s[slot, 0:halo, :] = carry_s[:, cols]
        ubuf_s[slot, halo:halo + tm, :] = u
        carry_s[:, cols] = u[tm - halo:tm, :]
        u1 = ubuf_s[slot, halo - 1:halo - 1 + tm, :]
        u2 = ubuf_s[slot, halo - 2:halo - 2 + tm, :]
        return (cb_ref[:, cols] + cw_ref[0:1, cols] * u2 + cw_ref[1:2, cols] * u1
                + cw_ref[2:3, cols] * u)

    n_ff = D_FF // FF_CHUNK
    head = (n_ff - 1) * FF_CHUNK
    for c in range(n_ff):
        gcols = slice(c * FF_CHUNK, (c + 1) * FF_CHUNK)
        vcols = slice(D_FF + c * FF_CHUNK, D_FF + (c + 1) * FF_CHUNK)
        d_gate = jnp.dot(h_s[...], wup_ref[:, gcols], preferred_element_type=F32)
        d_val = jnp.dot(h_s[...], wup_ref[:, vcols], preferred_element_type=F32)
        if c == n_ff - 1:
            out_ref[...] += jnp.dot(act_s[:, 0:head], wdn_ref[0:head, :],
                                    preferred_element_type=F32)
        u_gate = conv(d_gate, 0, gcols)
        u_val = conv(d_val, 1, vcols)
        act_s[:, gcols] = (u_gate * jax.nn.sigmoid(u_gate) * u_val).astype(BF16)
    out_ref[...] += jnp.dot(act_s[:, head:D_FF], wdn_ref[head:D_FF, :], preferred_element_type=F32)


def _mix_ffn(x2, o_r, o_m, attn_norm, w_gate, b_gate, w_ret_o, w_moba_o, w_out,
             ffn_norm, w_up, conv_w, conv_b, w_down, seq):
    tokens = x2.shape[0]
    tile = lambda cols: pl.BlockSpec((TOKEN_TILE, cols), lambda i: (i, 0))
    return pl.pallas_call(
        functools.partial(_mix_ffn_kernel, tiles_per_seq=seq // TOKEN_TILE),
        grid=(tokens // TOKEN_TILE,),
        in_specs=[
            tile(D_MODEL), tile(RET_V), tile(MOBA_W),
            _resident((1, D_MODEL)), _resident((D_MODEL, 2 * D_MODEL)), _resident((1, 2 * D_MODEL)),
            _resident((RET_V, D_MODEL)), _resident((MOBA_W, D_MODEL)), _resident((D_MODEL, D_MODEL)),
            _resident((1, D_MODEL)), _resident((D_MODEL, 2 * D_FF)),
            _resident((CONV_WIDTH, 2 * D_FF)), _resident((1, 2 * D_FF)), _resident((D_FF, D_MODEL)),
        ],
        out_specs=tile(D_MODEL),
        out_shape=jax.ShapeDtypeStruct((tokens, D_MODEL), F32),
        scratch_shapes=[
            pltpu.VMEM((TOKEN_TILE, D_MODEL), BF16),
            pltpu.VMEM((TOKEN_TILE, D_MODEL), BF16),
            pltpu.VMEM((TOKEN_TILE, D_FF), BF16),
            pltpu.VMEM((2, TOKEN_TILE + V7X_SUBLANES, FF_CHUNK), F32),
            pltpu.VMEM((V7X_SUBLANES, 2 * D_FF), F32),
        ],
        compiler_params=pltpu.CompilerParams(
            dimension_semantics=("arbitrary",), vmem_limit_bytes=V7X_VMEM_LIMIT_BYTES),
        name="merge_ffn",
    )(x2, o_r, o_m, attn_norm, w_gate, b_gate, w_ret_o, w_moba_o, w_out,
      ffn_norm, w_up, conv_w, conv_b, w_down)


def _rotary_tables(seq):
    half = RET_QK_DIM // 2
    inv_freq = ROPE_BASE ** (-jnp.arange(half, dtype=F32) / half)
    ang = jnp.arange(seq).astype(F32)[:, None] * inv_freq[None, :]
    cos, sin = jnp.cos(ang), jnp.sin(ang)
    return jnp.concatenate([cos, cos], axis=-1), jnp.concatenate([-sin, sin], axis=-1)


def _retention_tables():
    log_gamma = jnp.log1p(-jnp.power(2.0, -5.0 - jnp.arange(RET_HEADS, dtype=F32)))
    pos = jnp.arange(RET_CHUNK, dtype=F32)
    diff = pos[:, None] - pos[None, :]
    decay = jnp.where(diff >= 0, jnp.exp(log_gamma[:, None, None] * jnp.maximum(diff, 0.0)), 0.0)
    zeta = jnp.exp(log_gamma[:, None] * (RET_CHUNK - 1 - pos)[None, :])
    xi = jnp.exp(log_gamma[:, None] * (pos + 1)[None, :])
    gamma_c = jnp.exp(log_gamma * RET_CHUNK)
    xi_b = jnp.broadcast_to(xi[:, :, None], (RET_HEADS, RET_CHUNK, RET_V_DIM))
    zeta_b = jnp.broadcast_to(zeta[:, :, None], (RET_HEADS, RET_CHUNK, RET_QK_DIM))
    gamma_b = jnp.broadcast_to(gamma_c[:, None, None], (RET_HEADS, 1, RET_V_DIM))
    return decay, xi_b, zeta_b, gamma_b


def kernel(x, attn_norm, w_in, ret_norm_gain, ret_norm_bias, moba_q_gain, moba_k_gain,
           w_ret_o, w_moba_o, w_gate, b_gate, w_out, ffn_norm, w_up, conv_w, conv_b, w_down):
    batch, seq, d_model = x.shape
    depth = attn_norm.shape[0]
    assert d_model == D_MODEL and seq % TOKEN_TILE == 0 and seq % MOBA_BLOCK == 0
    cos, sin = _rotary_tables(seq)
    decay, xi_b, zeta_b, gamma_b = _retention_tables()
    x2 = x.reshape(batch * seq, d_model)
    for l in range(depth):
        q_gain = moba_q_gain[l][None] * (MOBA_HEAD_DIM ** -0.5 * LOG2_E)
        proj, k_mean, v_t, q_t = _in_projection(x2, attn_norm[l][None], cos, sin, q_gain,
                                                moba_k_gain[l][None], w_in[l].astype(BF16), seq)
        proj3 = proj.reshape(batch, seq, PROJ_W)
        o_r = _retention(proj3, decay, xi_b, zeta_b, gamma_b,
                         ret_norm_gain[l][None], ret_norm_bias[l][None])
        o_m = _moba(proj3, k_mean, v_t, q_t)
        x2 = _mix_ffn(
            x2, o_r.reshape(batch * seq, RET_V), o_m.reshape(batch * seq, MOBA_W),
            attn_norm[l][None], w_gate[l].astype(BF16), b_gate[l][None],
            w_ret_o[l].astype(BF16), w_moba_o[l].astype(BF16), w_out[l].astype(BF16),
            ffn_norm[l][None], w_up[l].astype(BF16), conv_w[l], conv_b[l][None],
            w_down[l].astype(BF16), seq)
    return x2.reshape(batch, seq, d_model)
```

```python
import functools

import jax
import jax.numpy as jnp
from jax import lax
from jax.experimental import pallas as pl
from jax.experimental.pallas import tpu as pltpu

F32 = jnp.float32
BF16 = jnp.bfloat16

D_MODEL = 1024
RET_HEADS = 4
RET_QK_DIM = 128
RET_V_DIM = 256
ROPE_BASE = 10000.0
MOBA_HEADS = 8
MOBA_HEAD_DIM = 128
MOBA_BLOCK = 256
MOBA_TOPK = 3
D_FF = 2816
CONV_WIDTH = 3
NORM_EPS = 1e-6
LOG2_E = 1.4426950408889634

RET_QK = RET_HEADS * RET_QK_DIM
RET_V = RET_HEADS * RET_V_DIM
MOBA_W = MOBA_HEADS * MOBA_HEAD_DIM
D_IN = 2 * RET_QK + 2 * RET_V + 3 * MOBA_W
OFF_QR, OFF_KR, OFF_VR, OFF_GR = 0, RET_QK, 2 * RET_QK, 2 * RET_QK + RET_V
OFF_QM = 2 * RET_QK + 2 * RET_V
OFF_KM, OFF_VM = OFF_QM + MOBA_W, OFF_QM + 2 * MOBA_W
PROJ_OFF_KM = OFF_QM
PROJ_W = PROJ_OFF_KM + MOBA_W

V7X_LANES = 128
V7X_SUBLANES = 8
V7X_VMEM_LIMIT_BYTES = 56 * 1024 * 1024

TOKEN_TILE = 512
PROJ_COL_CHUNK = 512
MIX_COL_CHUNK = 512
FF_CHUNK = 256
MOBA_HEADS_PER_STEP = 4
MOBA_VT_ROWS = MOBA_HEAD_DIM + 16
RET_HEADS_PER_STEP = 2
RET_CHUNK = 256


def _resident(shape):
    nd = len(shape)
    return pl.BlockSpec(shape, lambda *_: (0,) * nd, pipeline_mode=pl.Buffered(1))


def _rms(x, gain):
    inv = lax.rsqrt(jnp.mean(x * x, axis=-1, keepdims=True) + NORM_EPS)
    return (x * inv) * gain


def _inproj_kernel(x_ref, gain_ref, cos_ref, sin_ref, gq_ref, gk_ref, w_ref,
                   o_ref, km_ref, vt_ref, qt_ref, h_s):
    tm = x_ref.shape[0]
    h_s[...] = _rms(x_ref[...], gain_ref[...]).astype(BF16)
    cos = cos_ref[...]
    sin = sin_ref[...]
    half = RET_QK_DIM // 2
    starts = list(range(0, D_IN, PROJ_COL_CHUNK))
    order = ([c0 for c0 in starts if c0 >= OFF_QM][::-1]
             + [c0 for c0 in starts if c0 < OFF_VR]
             + [c0 for c0 in starts if OFF_GR <= c0 < OFF_QM]
             + [c0 for c0 in starts if OFF_VR <= c0 < OFF_GR])
    for c0 in order:
        acc = jnp.dot(h_s[...], w_ref[:, c0:c0 + PROJ_COL_CHUNK], preferred_element_type=F32)
        if c0 < OFF_VR:
            k_scale = RET_QK_DIM ** -0.5 if c0 >= OFF_KR else None
            for s in range(PROJ_COL_CHUNK // RET_QK_DIM):
                seg = acc[:, s * RET_QK_DIM:(s + 1) * RET_QK_DIM]
                seg = seg * cos + pltpu.roll(seg, half, axis=1) * sin
                if k_scale is not None:
                    seg = seg * k_scale
                o_ref[:, c0 + s * RET_QK_DIM:c0 + (s + 1) * RET_QK_DIM] = seg.astype(BF16)
        elif OFF_QM <= c0 < OFF_VM:
            is_key = c0 >= OFF_KM
            gain = gk_ref[...] if is_key else gq_ref[...]
            for s in range(PROJ_COL_CHUNK // MOBA_HEAD_DIM):
                seg = _rms(acc[:, s * MOBA_HEAD_DIM:(s + 1) * MOBA_HEAD_DIM], gain)
                if is_key:
                    head = (c0 - OFF_KM) // MOBA_HEAD_DIM + s
                    kcols = slice(head * MOBA_HEAD_DIM, (head + 1) * MOBA_HEAD_DIM)
                    o_ref[:, PROJ_OFF_KM + kcols.start:PROJ_OFF_KM + kcols.stop] = seg.astype(BF16)
                    km_ref[:, kcols] = jnp.mean(
                        seg.reshape(tm // MOBA_BLOCK, MOBA_BLOCK, MOBA_HEAD_DIM), axis=1)
                else:
                    head = (c0 - OFF_QM) // MOBA_HEAD_DIM + s
                    qt_ref[head, :, :] = seg.T.astype(BF16)
        elif c0 >= OFF_VM:
            for s in range(PROJ_COL_CHUNK // MOBA_HEAD_DIM):
                head = (c0 - OFF_VM) // MOBA_HEAD_DIM + s
                seg = acc[:, s * MOBA_HEAD_DIM:(s + 1) * MOBA_HEAD_DIM]
                vt_ref[head, 0:MOBA_HEAD_DIM, :] = seg.T.astype(BF16)
                vt_ref[head, MOBA_HEAD_DIM:MOBA_VT_ROWS, :] = jnp.ones(
                    (MOBA_VT_ROWS - MOBA_HEAD_DIM, tm), BF16)
        else:
            o_ref[:, c0:c0 + PROJ_COL_CHUNK] = acc.astype(BF16)


def _in_projection(x2, attn_norm, cos, sin, q_gain, k_gain, w_in, seq):
    tokens = x2.shape[0]
    tiles_per_seq = seq // TOKEN_TILE
    blocks_per_tile = TOKEN_TILE // MOBA_BLOCK
    proj, k_mean, v_t, q_t = pl.pallas_call(
        _inproj_kernel,
        grid=(tokens // TOKEN_TILE,),
        in_specs=[
            pl.BlockSpec((TOKEN_TILE, D_MODEL), lambda i: (i, 0)),
            _resident((1, D_MODEL)),
            pl.BlockSpec((TOKEN_TILE, RET_QK_DIM), lambda i: (i % tiles_per_seq, 0)),
            pl.BlockSpec((TOKEN_TILE, RET_QK_DIM), lambda i: (i % tiles_per_seq, 0)),
            _resident((1, MOBA_HEAD_DIM)),
            _resident((1, MOBA_HEAD_DIM)),
            _resident((D_MODEL, D_IN)),
        ],
        out_specs=[
            pl.BlockSpec((TOKEN_TILE, PROJ_W), lambda i: (i, 0)),
            pl.BlockSpec((None, blocks_per_tile, MOBA_W), lambda i: (i, 0, 0)),
            pl.BlockSpec((MOBA_HEADS, MOBA_VT_ROWS, TOKEN_TILE), lambda i: (0, 0, i)),
            pl.BlockSpec((MOBA_HEADS, MOBA_HEAD_DIM, TOKEN_TILE), lambda i: (0, 0, i)),
        ],
        out_shape=[
            jax.ShapeDtypeStruct((tokens, PROJ_W), BF16),
            jax.ShapeDtypeStruct((tokens // TOKEN_TILE, blocks_per_tile, MOBA_W), F32),
            jax.ShapeDtypeStruct((MOBA_HEADS, MOBA_VT_ROWS, tokens), BF16),
            jax.ShapeDtypeStruct((MOBA_HEADS, MOBA_HEAD_DIM, tokens), BF16),
        ],
        scratch_shapes=[pltpu.VMEM((TOKEN_TILE, D_MODEL), BF16)],
        compiler_params=pltpu.CompilerParams(
            dimension_semantics=("arbitrary",), vmem_limit_bytes=V7X_VMEM_LIMIT_BYTES),
        name="in_projection",
    )(x2, attn_norm, cos, sin, q_gain, k_gain, w_in)
    return proj, k_mean.reshape(tokens // MOBA_BLOCK, MOBA_W), v_t, q_t


def _retention_kernel(q_ref, k_ref, v_ref, g_ref, dec_ref, xi_ref, zeta_ref, gc_ref,
                      gain_ref, bias_ref, o_ref):
    seq = q_ref.shape[0]
    n_heads = dec_ref.shape[0]
    state = [jnp.zeros((RET_QK_DIM, RET_V_DIM), F32) for _ in range(n_heads)]
    for c in range(seq // RET_CHUNK):
        rows = slice(c * RET_CHUNK, (c + 1) * RET_CHUNK)
        for h in range(n_heads):
            qk_cols = slice(h * RET_QK_DIM, (h + 1) * RET_QK_DIM)
            v_cols = slice(h * RET_V_DIM, (h + 1) * RET_V_DIM)
            qc, kc, vc = q_ref[rows, qk_cols], k_ref[rows, qk_cols], v_ref[rows, v_cols]
            scores = lax.dot_general(qc, kc, (((1,), (1,)), ((), ())),
                                     preferred_element_type=F32) * dec_ref[h]
            o = jnp.dot(scores.astype(BF16), vc, preferred_element_type=F32)
            o = o + jnp.dot(qc, state[h].astype(BF16), preferred_element_type=F32) * xi_ref[h]
            kz = (kc.astype(F32) * zeta_ref[h]).astype(BF16)
            state[h] = gc_ref[h] * state[h] + lax.dot_general(
                kz, vc, (((0,), (0,)), ((), ())), preferred_element_type=F32)
            mu = jnp.mean(o, axis=-1, keepdims=True)
            d = o - mu
            var = jnp.mean(d * d, axis=-1, keepdims=True)
            on = d * lax.rsqrt(var + NORM_EPS)
            g = g_ref[rows, v_cols].astype(F32)
            o_ref[rows, v_cols] = ((on * gain_ref[:, v_cols] + bias_ref[:, v_cols])
                                   * (g * jax.nn.sigmoid(g))).astype(BF16)


def _retention(proj3, decay, xi, zeta, gamma_c, gain, bias):
    batch, seq, _ = proj3.shape
    hps = RET_HEADS_PER_STEP
    qk_w, v_w = hps * RET_QK_DIM, hps * RET_V_DIM
    qk_blk = lambda off: pl.BlockSpec((None, seq, qk_w), lambda b, g: (b, 0, off // qk_w + g))
    v_blk = lambda off: pl.BlockSpec((None, seq, v_w), lambda b, g: (b, 0, off // v_w + g))
    per_head = lambda r, c: pl.BlockSpec((hps, r, c), lambda b, g: (g, 0, 0))
    return pl.pallas_call(
        _retention_kernel,
        grid=(batch, RET_HEADS // hps),
        in_specs=[
            qk_blk(OFF_QR), qk_blk(OFF_KR), v_blk(OFF_VR), v_blk(OFF_GR),
            per_head(RET_CHUNK, RET_CHUNK), per_head(RET_CHUNK, RET_V_DIM),
            per_head(RET_CHUNK, RET_QK_DIM), per_head(1, RET_V_DIM),
            pl.BlockSpec((1, v_w), lambda b, g: (0, g)),
            pl.BlockSpec((1, v_w), lambda b, g: (0, g)),
        ],
        out_specs=pl.BlockSpec((None, seq, v_w), lambda b, g: (b, 0, g)),
        out_shape=jax.ShapeDtypeStruct((batch, seq, RET_V), BF16),
        compiler_params=pltpu.CompilerParams(
            dimension_semantics=("arbitrary", "arbitrary"), vmem_limit_bytes=V7X_VMEM_LIMIT_BYTES),
        name="retention",
    )(proj3, proj3, proj3, proj3, decay, xi, zeta, gamma_c, gain, bias)


def _moba_kernel(qt_ref, k_ref, vt_ref, km_ref, o_ref, km3_s, s_s, p_s):
    seq = k_ref.shape[0]
    n_heads = vt_ref.shape[0]
    nb = seq // MOBA_BLOCK
    km = km_ref[...]
    km_hi = km.astype(BF16)
    km_mid = (km - km_hi.astype(F32)).astype(BF16)
    km_lo = (km - km_hi.astype(F32) - km_mid.astype(F32)).astype(BF16)
    km3_s[...] = jnp.concatenate([km_hi, km_mid, km_lo, jnp.zeros_like(km_hi)], axis=0)

    key_i = lax.broadcasted_iota(jnp.int32, (MOBA_BLOCK, MOBA_BLOCK), 0)
    qry_i = lax.broadcasted_iota(jnp.int32, (MOBA_BLOCK, MOBA_BLOCK), 1)
    causal = key_i <= qry_i

    def blk(j):
        return slice(j * MOBA_BLOCK, (j + 1) * MOBA_BLOCK)

    def hcols(h):
        return slice(h * MOBA_HEAD_DIM, (h + 1) * MOBA_HEAD_DIM)

    def keep_masks(h, qb):
        if qb <= MOBA_TOPK:
            return [None] * qb
        parts = jnp.dot(km3_s[:, hcols(h)], qt_ref[h, :, blk(qb)],
                        preferred_element_type=F32)
        gate = parts[0:nb] + (parts[nb:2 * nb] + parts[2 * nb:3 * nb])
        g = [gate[i:i + 1, :] for i in range(qb)]
        keep = []
        for j in range(qb):
            beaten = jnp.zeros((1, MOBA_BLOCK), jnp.int32)
            for i in range(qb):
                if i < j:
                    beaten = beaten + (g[i] >= g[j]).astype(jnp.int32)
                elif i > j:
                    beaten = beaten + (g[i] > g[j]).astype(jnp.int32)
            keep.append(beaten < MOBA_TOPK)
        return keep

    def scores_stage(h, qb):
        slot = qb % 2
        keep = keep_masks(h, qb)
        q_blk_t = qt_ref[h, :, blk(qb)]
        m = None
        for j in range(qb + 1):
            s = jnp.dot(k_ref[blk(j), hcols(h)], q_blk_t,
                        preferred_element_type=F32)
            if j == qb:
                s = jnp.where(causal, s, -jnp.inf)
            elif keep[j] is not None:
                s = jnp.where(keep[j], s, -jnp.inf)
            s_s[h, slot, blk(j), :] = s
            mj = s_s[h, slot, blk(j), :].max(axis=0, keepdims=True)
            m = mj if m is None else jnp.maximum(m, mj)
        return m

    def softmax_stage(h, qb, m):
        slot = qb % 2
        for j in range(qb + 1):
            p_s[h, slot, blk(j), :] = jnp.exp2(s_s[h, slot, blk(j), :] - m).astype(BF16)

    def value_stage(h, qb):
        slot = qb % 2
        nk = (qb + 1) * MOBA_BLOCK
        o_t = jnp.dot(vt_ref[h, :, 0:nk], p_s[h, slot, 0:nk, :],
                      preferred_element_type=F32)
        denom = o_t[MOBA_HEAD_DIM:MOBA_HEAD_DIM + 1, :]
        o_ref[blk(qb), hcols(h)] = (o_t[0:MOBA_HEAD_DIM, :] / denom).T.astype(BF16)

    heads = range(n_heads)
    m_next = [scores_stage(h, 0) for h in heads]
    for qb in range(nb):
        m_cur = m_next
        if qb + 1 < nb:
            m_next = [scores_stage(h, qb + 1) for h in heads]
        for h in heads:
            softmax_stage(h, qb, m_cur[h])
            value_stage(h, qb)


def _moba(proj3, k_mean, v_t, q_t):
    batch, seq, _ = proj3.shape
    nb = seq // MOBA_BLOCK
    hps = MOBA_HEADS_PER_STEP
    width = hps * MOBA_HEAD_DIM
    return pl.pallas_call(
        _moba_kernel,
        grid=(batch, MOBA_HEADS // hps),
        in_specs=[
            pl.BlockSpec((hps, MOBA_HEAD_DIM, seq), lambda b, g: (g, 0, b)),
            pl.BlockSpec((None, seq, width), lambda b, g: (b, 0, PROJ_OFF_KM // width + g)),
            pl.BlockSpec((hps, MOBA_VT_ROWS, seq), lambda b, g: (g, 0, b)),
            pl.BlockSpec((nb, width), lambda b, g: (b, g)),
        ],
        out_specs=pl.BlockSpec((None, seq, width), lambda b, g: (b, 0, g)),
        out_shape=jax.ShapeDtypeStruct((batch, seq, MOBA_W), BF16),
        scratch_shapes=[
            pltpu.VMEM((4 * nb, width), BF16),
            pltpu.VMEM((hps, 2, seq, MOBA_BLOCK), F32),
            pltpu.VMEM((hps, 2, seq, MOBA_BLOCK), BF16),
        ],
        compiler_params=pltpu.CompilerParams(
            dimension_semantics=("arbitrary", "arbitrary"), vmem_limit_bytes=V7X_VMEM_LIMIT_BYTES),
        name="block_attention",
    )(q_t, proj3, v_t, k_mean)


def _mix_ffn_kernel(x_ref, or_ref, om_ref, an_ref, wg_ref, bg_ref, wro_ref, wmo_ref, wo_ref,
                    fn_ref, wup_ref, cw_ref, cb_ref, wdn_ref, out_ref,
                    h_s, merged_s, act_s, ubuf_s, carry_s, *, tiles_per_seq):
    tm = x_ref.shape[0]
    halo = V7X_SUBLANES

    @pl.when(pl.program_id(0) % tiles_per_seq == 0)
    def _():
        carry_s[...] = jnp.zeros_like(carry_s)

    x = x_ref[...]
    h_s[...] = _rms(x, an_ref[...]).astype(BF16)
    for c in range(D_MODEL // MIX_COL_CHUNK):
        cs = slice(c * MIX_COL_CHUNK, (c + 1) * MIX_COL_CHUNK)
        cs2 = slice(D_MODEL + c * MIX_COL_CHUNK, D_MODEL + (c + 1) * MIX_COL_CHUNK)
        y_ret = jnp.dot(or_ref[...], wro_ref[:, cs], preferred_element_type=F32)
        y_moba = jnp.dot(om_ref[...], wmo_ref[:, cs], preferred_element_type=F32)
        g_ret = jax.nn.sigmoid(
            jnp.dot(h_s[...], wg_ref[:, cs], preferred_element_type=F32) + bg_ref[:, cs])
        g_moba = jax.nn.sigmoid(
            jnp.dot(h_s[...], wg_ref[:, cs2], preferred_element_type=F32) + bg_ref[:, cs2])
        merged_s[:, cs] = (g_ret * y_ret + g_moba * y_moba).astype(BF16)
    x1 = x + jnp.dot(merged_s[...], wo_ref[...], preferred_element_type=F32)
    out_ref[...] = x1
    h_s[...] = _rms(x1, fn_ref[...]).astype(BF16)

    def conv(u, slot, cols):
        ubuf_s[slot, 0:halo, :] = carry_s[:, cols]
        ubuf_s[slot, halo:halo + tm, :] = u
        carry_s[:, cols] = u[tm - halo:tm, :]
        u1 = ubuf_s[slot, halo - 1:halo - 1 + tm, :]
        u2 = ubuf_s[slot, halo - 2:halo - 2 + tm, :]
        return (cb_ref[:, cols] + cw_ref[0:1, cols] * u2 + cw_ref[1:2, cols] * u1
                + cw_ref[2:3, cols] * u)

    n_ff = D_FF // FF_CHUNK
    head = (n_ff - 1) * FF_CHUNK
    for c in range(n_ff):
        gcols = slice(c * FF_CHUNK, (c + 1) * FF_CHUNK)
        vcols = slice(D_FF + c * FF_CHUNK, D_FF + (c + 1) * FF_CHUNK)
        d_gate = jnp.dot(h_s[...], wup_ref[:, gcols], preferred_element_type=F32)
        d_val = jnp.dot(h_s[...], wup_ref[:, vcols], preferred_element_type=F32)
        if c == n_ff - 1:
            out_ref[...] += jnp.dot(act_s[:, 0:head], wdn_ref[0:head, :],
                                    preferred_element_type=F32)
        for p in range(FF_CHUNK // V7X_LANES):
            piece = slice(p * V7X_LANES, (p + 1) * V7X_LANES)
            g_piece = slice(gcols.start + piece.start, gcols.start + piece.stop)
            v_piece = slice(vcols.start + piece.start, vcols.start + piece.stop)
            u_gate = conv(d_gate[:, piece], 2 * p, g_piece)
            u_val = conv(d_val[:, piece], 2 * p + 1, v_piece)
            act_s[:, g_piece] = (u_gate * jax.nn.sigmoid(u_gate) * u_val).astype(BF16)
    out_ref[...] += jnp.dot(act_s[:, head:D_FF], wdn_ref[head:D_FF, :], preferred_element_type=F32)


def _mix_ffn(x2, o_r, o_m, attn_norm, w_gate, b_gate, w_ret_o, w_moba_o, w_out,
             ffn_norm, w_up, conv_w, conv_b, w_down, seq):
    tokens = x2.shape[0]
    tile = lambda cols: pl.BlockSpec((TOKEN_TILE, cols), lambda i: (i, 0))
    return pl.pallas_call(
        functools.partial(_mix_ffn_kernel, tiles_per_seq=seq // TOKEN_TILE),
        grid=(tokens // TOKEN_TILE,),
        in_specs=[
            tile(D_MODEL), tile(RET_V), tile(MOBA_W),
            _resident((1, D_MODEL)), _resident((D_MODEL, 2 * D_MODEL)), _resident((1, 2 * D_MODEL)),
            _resident((RET_V, D_MODEL)), _resident((MOBA_W, D_MODEL)), _resident((D_MODEL, D_MODEL)),
            _resident((1, D_MODEL)), _resident((D_MODEL, 2 * D_FF)),
            _resident((CONV_WIDTH, 2 * D_FF)), _resident((1, 2 * D_FF)), _resident((D_FF, D_MODEL)),
        ],
        out_specs=tile(D_MODEL),
        out_shape=jax.ShapeDtypeStruct((tokens, D_MODEL), F32),
        scratch_shapes=[
            pltpu.VMEM((TOKEN_TILE, D_MODEL), BF16),
            pltpu.VMEM((TOKEN_TILE, D_MODEL), BF16),
            pltpu.VMEM((TOKEN_TILE, D_FF), BF16),
            pltpu.VMEM((2 * FF_CHUNK // V7X_LANES, TOKEN_TILE + V7X_SUBLANES, V7X_LANES), F32),
            pltpu.VMEM((V7X_SUBLANES, 2 * D_FF), F32),
        ],
        compiler_params=pltpu.CompilerParams(
            dimension_semantics=("arbitrary",), vmem_limit_bytes=V7X_VMEM_LIMIT_BYTES),
        name="merge_ffn",
    )(x2, o_r, o_m, attn_norm, w_gate, b_gate, w_ret_o, w_moba_o, w_out,
      ffn_norm, w_up, conv_w, conv_b, w_down)


def _rotary_tables(seq):
    half = RET_QK_DIM // 2
    inv_freq = ROPE_BASE ** (-jnp.arange(half, dtype=F32) / half)
    ang = jnp.arange(seq).astype(F32)[:, None] * inv_freq[None, :]
    cos, sin = jnp.cos(ang), jnp.sin(ang)
    return jnp.concatenate([cos, cos], axis=-1), jnp.concatenate([-sin, sin], axis=-1)


def _retention_tables():
    log_gamma = jnp.log1p(-jnp.power(2.0, -5.0 - jnp.arange(RET_HEADS, dtype=F32)))
    pos = jnp.arange(RET_CHUNK, dtype=F32)
    diff = pos[:, None] - pos[None, :]
    decay = jnp.where(diff >= 0, jnp.exp(log_gamma[:, None, None] * jnp.maximum(diff, 0.0)), 0.0)
    zeta = jnp.exp(log_gamma[:, None] * (RET_CHUNK - 1 - pos)[None, :])
    xi = jnp.exp(log_gamma[:, None] * (pos + 1)[None, :])
    gamma_c = jnp.exp(log_gamma * RET_CHUNK)
    xi_b = jnp.broadcast_to(xi[:, :, None], (RET_HEADS, RET_CHUNK, RET_V_DIM))
    zeta_b = jnp.broadcast_to(zeta[:, :, None], (RET_HEADS, RET_CHUNK, RET_QK_DIM))
    gamma_b = jnp.broadcast_to(gamma_c[:, None, None], (RET_HEADS, 1, RET_V_DIM))
    return decay, xi_b, zeta_b, gamma_b


def kernel(x, attn_norm, w_in, ret_norm_gain, ret_norm_bias, moba_q_gain, moba_k_gain,
           w_ret_o, w_moba_o, w_gate, b_gate, w_out, ffn_norm, w_up, conv_w, conv_b, w_down):
    batch, seq, d_model = x.shape
    depth = attn_norm.shape[0]
    assert d_model == D_MODEL and seq % TOKEN_TILE == 0 and seq % MOBA_BLOCK == 0
    cos, sin = _rotary_tables(seq)
    decay, xi_b, zeta_b, gamma_b = _retention_tables()
    x2 = x.reshape(batch * seq, d_model)
    for l in range(depth):
        q_gain = moba_q_gain[l][None] * (MOBA_HEAD_DIM ** -0.5 * LOG2_E)
        proj, k_mean, v_t, q_t = _in_projection(x2, attn_norm[l][None], cos, sin, q_gain,
                                                moba_k_gain[l][None], w_in[l].astype(BF16), seq)
        proj3 = proj.reshape(batch, seq, PROJ_W)
        o_r = _retention(proj3, decay, xi_b, zeta_b, gamma_b,
                         ret_norm_gain[l][None], ret_norm_bias[l][None])
        o_m = _moba(proj3, k_mean, v_t, q_t)
        x2 = _mix_ffn(
            x2, o_r.reshape(batch * seq, RET_V), o_m.reshape(batch * seq, MOBA_W),
            attn_norm[l][None], w_gate[l].astype(BF16), b_gate[l][None],
            w_ret_o[l].astype(BF16), w_moba_o[l].astype(BF16), w_out[l].astype(BF16),
            ffn_norm[l][None], w_up[l].astype(BF16), conv_w[l], conv_b[l][None],
            w_down[l].astype(BF16), seq)
    return x2.reshape(batch, seq, d_model)
```

```python
import functools

import jax
import jax.numpy as jnp
from jax import lax
from jax.experimental import pallas as pl
from jax.experimental.pallas import tpu as pltpu

F32 = jnp.float32
BF16 = jnp.bfloat16

D_MODEL = 1024
RET_HEADS = 4
RET_QK_DIM = 128
RET_V_DIM = 256
ROPE_BASE = 10000.0
MOBA_HEADS = 8
MOBA_HEAD_DIM = 128
MOBA_BLOCK = 256
MOBA_TOPK = 3
D_FF = 2816
CONV_WIDTH = 3
NORM_EPS = 1e-6
LOG2_E = 1.4426950408889634

RET_QK = RET_HEADS * RET_QK_DIM
RET_V = RET_HEADS * RET_V_DIM
MOBA_W = MOBA_HEADS * MOBA_HEAD_DIM
D_IN = 2 * RET_QK + 2 * RET_V + 3 * MOBA_W
OFF_QR, OFF_KR, OFF_VR, OFF_GR = 0, RET_QK, 2 * RET_QK, 2 * RET_QK + RET_V
OFF_QM = 2 * RET_QK + 2 * RET_V
OFF_KM, OFF_VM = OFF_QM + MOBA_W, OFF_QM + 2 * MOBA_W
PROJ_OFF_KM = OFF_QM
PROJ_W = PROJ_OFF_KM + MOBA_W

V7X_LANES = 128
V7X_SUBLANES = 8
V7X_BF16_SUBLANE_TILE = 16
V7X_VMEM_LIMIT_BYTES = 56 * 1024 * 1024

TOKEN_TILE = 512
PROJ_COL_CHUNK = 512
MIX_COL_CHUNK = 512
WEIGHT_CAST_STEPS = 16
FF_CHUNK = 256
MOBA_HEADS_PER_STEP = 4
MOBA_VT_ROWS = MOBA_HEAD_DIM + 16
RET_HEADS_PER_STEP = 2
RET_CHUNK = 256


def _resident(shape):
    nd = len(shape)
    return pl.BlockSpec(shape, lambda *_: (0,) * nd, pipeline_mode=pl.Buffered(1))


def _rms(x, gain):
    inv = lax.rsqrt(jnp.mean(x * x, axis=-1, keepdims=True) + NORM_EPS)
    return (x * inv) * gain


def _inproj_kernel(x_ref, gain_ref, cos_ref, sin_ref, gq_ref, gk_ref, w_ref,
                   o_ref, km_ref, vt_ref, qt_ref, h_s):
    tm = x_ref.shape[0]
    h_s[...] = _rms(x_ref[...], gain_ref[...]).astype(BF16)
    cos = cos_ref[...]
    sin = sin_ref[...]
    half = RET_QK_DIM // 2
    starts = list(range(0, D_IN, PROJ_COL_CHUNK))
    order = ([c0 for c0 in starts if c0 >= OFF_QM][::-1]
             + [c0 for c0 in starts if c0 < OFF_VR]
             + [c0 for c0 in starts if OFF_GR <= c0 < OFF_QM]
             + [c0 for c0 in starts if OFF_VR <= c0 < OFF_GR])
    for c0 in order:
        acc = jnp.dot(h_s[...], w_ref[:, c0:c0 + PROJ_COL_CHUNK], preferred_element_type=F32)
        if c0 < OFF_VR:
            k_scale = RET_QK_DIM ** -0.5 if c0 >= OFF_KR else None
            for s in range(PROJ_COL_CHUNK // RET_QK_DIM):
                seg = acc[:, s * RET_QK_DIM:(s + 1) * RET_QK_DIM]
                seg = seg * cos + pltpu.roll(seg, half, axis=1) * sin
                if k_scale is not None:
                    seg = seg * k_scale
                o_ref[:, c0 + s * RET_QK_DIM:c0 + (s + 1) * RET_QK_DIM] = seg.astype(BF16)
        elif OFF_QM <= c0 < OFF_VM:
            is_key = c0 >= OFF_KM
            gain = gk_ref[...] if is_key else gq_ref[...]
            for s in range(PROJ_COL_CHUNK // MOBA_HEAD_DIM):
                seg = _rms(acc[:, s * MOBA_HEAD_DIM:(s + 1) * MOBA_HEAD_DIM], gain)
                if is_key:
                    head = (c0 - OFF_KM) // MOBA_HEAD_DIM + s
                    kcols = slice(head * MOBA_HEAD_DIM, (head + 1) * MOBA_HEAD_DIM)
                    o_ref[:, PROJ_OFF_KM + kcols.start:PROJ_OFF_KM + kcols.stop] = seg.astype(BF16)
                    km_ref[:, kcols] = jnp.mean(
                        seg.reshape(tm // MOBA_BLOCK, MOBA_BLOCK, MOBA_HEAD_DIM), axis=1)
                else:
                    head = (c0 - OFF_QM) // MOBA_HEAD_DIM + s
                    qt_ref[head, :, :] = seg.T.astype(BF16)
        elif c0 >= OFF_VM:
            for s in range(PROJ_COL_CHUNK // MOBA_HEAD_DIM):
                head = (c0 - OFF_VM) // MOBA_HEAD_DIM + s
                seg = acc[:, s * MOBA_HEAD_DIM:(s + 1) * MOBA_HEAD_DIM]
                vt_ref[head, 0:MOBA_HEAD_DIM, :] = seg.T.astype(BF16)
                vt_ref[head, MOBA_HEAD_DIM:MOBA_VT_ROWS, :] = jnp.ones(
                    (MOBA_VT_ROWS - MOBA_HEAD_DIM, tm), BF16)
        else:
            o_ref[:, c0:c0 + PROJ_COL_CHUNK] = acc.astype(BF16)


def _in_projection(x2, attn_norm, cos, sin, q_gain, k_gain, w_in, seq):
    tokens = x2.shape[0]
    tiles_per_seq = seq // TOKEN_TILE
    blocks_per_tile = TOKEN_TILE // MOBA_BLOCK
    proj, k_mean, v_t, q_t = pl.pallas_call(
        _inproj_kernel,
        grid=(tokens // TOKEN_TILE,),
        in_specs=[
            pl.BlockSpec((TOKEN_TILE, D_MODEL), lambda i: (i, 0)),
            _resident((1, D_MODEL)),
            pl.BlockSpec((TOKEN_TILE, RET_QK_DIM), lambda i: (i % tiles_per_seq, 0)),
            pl.BlockSpec((TOKEN_TILE, RET_QK_DIM), lambda i: (i % tiles_per_seq, 0)),
            _resident((1, MOBA_HEAD_DIM)),
            _resident((1, MOBA_HEAD_DIM)),
            _resident((D_MODEL, D_IN)),
        ],
        out_specs=[
            pl.BlockSpec((TOKEN_TILE, PROJ_W), lambda i: (i, 0)),
            pl.BlockSpec((None, blocks_per_tile, MOBA_W), lambda i: (i, 0, 0)),
            pl.BlockSpec((MOBA_HEADS, MOBA_VT_ROWS, TOKEN_TILE), lambda i: (0, 0, i)),
            pl.BlockSpec((MOBA_HEADS, MOBA_HEAD_DIM, TOKEN_TILE), lambda i: (0, 0, i)),
        ],
        out_shape=[
            jax.ShapeDtypeStruct((tokens, PROJ_W), BF16),
            jax.ShapeDtypeStruct((tokens // TOKEN_TILE, blocks_per_tile, MOBA_W), F32),
            jax.ShapeDtypeStruct((MOBA_HEADS, MOBA_VT_ROWS, tokens), BF16),
            jax.ShapeDtypeStruct((MOBA_HEADS, MOBA_HEAD_DIM, tokens), BF16),
        ],
        scratch_shapes=[pltpu.VMEM((TOKEN_TILE, D_MODEL), BF16)],
        compiler_params=pltpu.CompilerParams(
            dimension_semantics=("arbitrary",), vmem_limit_bytes=V7X_VMEM_LIMIT_BYTES),
        name="in_projection",
    )(x2, attn_norm, cos, sin, q_gain, k_gain, w_in)
    return proj, k_mean.reshape(tokens // MOBA_BLOCK, MOBA_W), v_t, q_t


def _retention_kernel(q_ref, k_ref, v_ref, g_ref, dec_ref, xi_ref, zeta_ref, gc_ref,
                      gain_ref, bias_ref, o_ref):
    seq = q_ref.shape[0]
    n_heads = dec_ref.shape[0]
    state = [jnp.zeros((RET_QK_DIM, RET_V_DIM), F32) for _ in range(n_heads)]
    for c in range(seq // RET_CHUNK):
        rows = slice(c * RET_CHUNK, (c + 1) * RET_CHUNK)
        for h in range(n_heads):
            qk_cols = slice(h * RET_QK_DIM, (h + 1) * RET_QK_DIM)
            v_cols = slice(h * RET_V_DIM, (h + 1) * RET_V_DIM)
            qc, kc, vc = q_ref[rows, qk_cols], k_ref[rows, qk_cols], v_ref[rows, v_cols]
            scores = lax.dot_general(qc, kc, (((1,), (1,)), ((), ())),
                                     preferred_element_type=F32) * dec_ref[h]
            o = jnp.dot(scores.astype(BF16), vc, preferred_element_type=F32)
            o = o + jnp.dot(qc, state[h].astype(BF16), preferred_element_type=F32) * xi_ref[h]
            kz = (kc.astype(F32) * zeta_ref[h]).astype(BF16)
            state[h] = gc_ref[h] * state[h] + lax.dot_general(
                kz, vc, (((0,), (0,)), ((), ())), preferred_element_type=F32)
            mu = jnp.mean(o, axis=-1, keepdims=True)
            d = o - mu
            var = jnp.mean(d * d, axis=-1, keepdims=True)
            on = d * lax.rsqrt(var + NORM_EPS)
            g = g_ref[rows, v_cols].astype(F32)
            o_ref[rows, v_cols] = ((on * gain_ref[:, v_cols] + bias_ref[:, v_cols])
                                   * (g * jax.nn.sigmoid(g))).astype(BF16)


def _retention(proj3, decay, xi, zeta, gamma_c, gain, bias):
    batch, seq, _ = proj3.shape
    hps = RET_HEADS_PER_STEP
    qk_w, v_w = hps * RET_QK_DIM, hps * RET_V_DIM
    qk_blk = lambda off: pl.BlockSpec((None, seq, qk_w), lambda b, g: (b, 0, off // qk_w + g))
    v_blk = lambda off: pl.BlockSpec((None, seq, v_w), lambda b, g: (b, 0, off // v_w + g))
    per_head = lambda r, c: pl.BlockSpec((hps, r, c), lambda b, g: (g, 0, 0))
    return pl.pallas_call(
        _retention_kernel,
        grid=(batch, RET_HEADS // hps),
        in_specs=[
            qk_blk(OFF_QR), qk_blk(OFF_KR), v_blk(OFF_VR), v_blk(OFF_GR),
            per_head(RET_CHUNK, RET_CHUNK), per_head(RET_CHUNK, RET_V_DIM),
            per_head(RET_CHUNK, RET_QK_DIM), per_head(1, RET_V_DIM),
            pl.BlockSpec((1, v_w), lambda b, g: (0, g)),
            pl.BlockSpec((1, v_w), lambda b, g: (0, g)),
        ],
        out_specs=pl.BlockSpec((None, seq, v_w), lambda b, g: (b, 0, g)),
        out_shape=jax.ShapeDtypeStruct((batch, seq, RET_V), BF16),
        compiler_params=pltpu.CompilerParams(
            dimension_semantics=("arbitrary", "arbitrary"), vmem_limit_bytes=V7X_VMEM_LIMIT_BYTES),
        name="retention",
    )(proj3, proj3, proj3, proj3, decay, xi, zeta, gamma_c, gain, bias)


def _moba_kernel(qt_ref, k_ref, vt_ref, km_ref, o_ref, km3_s, s_s, p_s):
    seq = k_ref.shape[0]
    n_heads = vt_ref.shape[0]
    nb = seq // MOBA_BLOCK
    km = km_ref[...]
    km_hi = km.astype(BF16)
    km_mid = (km - km_hi.astype(F32)).astype(BF16)
    km_lo = (km - km_hi.astype(F32) - km_mid.astype(F32)).astype(BF16)
    km3_s[...] = jnp.concatenate([km_hi, km_mid, km_lo, jnp.zeros_like(km_hi)], axis=0)

    key_i = lax.broadcasted_iota(jnp.int32, (MOBA_BLOCK, MOBA_BLOCK), 0)
    qry_i = lax.broadcasted_iota(jnp.int32, (MOBA_BLOCK, MOBA_BLOCK), 1)
    causal = key_i <= qry_i

    def blk(j):
        return slice(j * MOBA_BLOCK, (j + 1) * MOBA_BLOCK)

    def hcols(h):
        return slice(h * MOBA_HEAD_DIM, (h + 1) * MOBA_HEAD_DIM)

    def keep_masks(h, qb):
        if qb <= MOBA_TOPK:
            return [None] * qb
        parts = jnp.dot(km3_s[:, hcols(h)], qt_ref[h, :, blk(qb)],
                        preferred_element_type=F32)
        gate = parts[0:nb] + (parts[nb:2 * nb] + parts[2 * nb:3 * nb])
        g = [gate[i:i + 1, :] for i in range(qb)]
        keep = []
        for j in range(qb):
            beaten = jnp.zeros((1, MOBA_BLOCK), jnp.int32)
            for i in range(qb):
                if i < j:
                    beaten = beaten + (g[i] >= g[j]).astype(jnp.int32)
                elif i > j:
                    beaten = beaten + (g[i] > g[j]).astype(jnp.int32)
            keep.append(beaten < MOBA_TOPK)
        return keep

    def scores_stage(h, qb):
        slot = qb % 2
        keep = keep_masks(h, qb)
        q_blk_t = qt_ref[h, :, blk(qb)]
        m = None
        for j in range(qb + 1):
            s = jnp.dot(k_ref[blk(j), hcols(h)], q_blk_t,
                        preferred_element_type=F32)
            if j == qb:
                s = jnp.where(causal, s, -jnp.inf)
            elif keep[j] is not None:
                s = jnp.where(keep[j], s, -jnp.inf)
            s_s[h, slot, blk(j), :] = s
            mj = s_s[h, slot, blk(j), :].max(axis=0, keepdims=True)
            m = mj if m is None else jnp.maximum(m, mj)
        return m

    def softmax_stage(h, qb, m):
        slot = qb % 2
        for j in range(qb + 1):
            p_s[h, slot, blk(j), :] = jnp.exp2(s_s[h, slot, blk(j), :] - m).astype(BF16)

    def value_stage(h, qb):
        slot = qb % 2
        nk = (qb + 1) * MOBA_BLOCK
        o_t = jnp.dot(vt_ref[h, :, 0:nk], p_s[h, slot, 0:nk, :],
                      preferred_element_type=F32)
        denom = o_t[MOBA_HEAD_DIM:MOBA_HEAD_DIM + 1, :]
        o_ref[blk(qb), hcols(h)] = (o_t[0:MOBA_HEAD_DIM, :] / denom).T.astype(BF16)

    heads = range(n_heads)
    m_next = [scores_stage(h, 0) for h in heads]
    for qb in range(nb):
        m_cur = m_next
        if qb + 1 < nb:
            m_next = [scores_stage(h, qb + 1) for h in heads]
        for h in heads:
            softmax_stage(h, qb, m_cur[h])
            value_stage(h, qb)


def _moba(proj3, k_mean, v_t, q_t):
    batch, seq, _ = proj3.shape
    nb = seq // MOBA_BLOCK
    hps = MOBA_HEADS_PER_STEP
    width = hps * MOBA_HEAD_DIM
    return pl.pallas_call(
        _moba_kernel,
        grid=(batch, MOBA_HEADS // hps),
        in_specs=[
            pl.BlockSpec((hps, MOBA_HEAD_DIM, seq), lambda b, g: (g, 0, b)),
            pl.BlockSpec((None, seq, width), lambda b, g: (b, 0, PROJ_OFF_KM // width + g)),
            pl.BlockSpec((hps, MOBA_VT_ROWS, seq), lambda b, g: (g, 0, b)),
            pl.BlockSpec((nb, width), lambda b, g: (b, g)),
        ],
        out_specs=pl.BlockSpec((None, seq, width), lambda b, g: (b, 0, g)),
        out_shape=jax.ShapeDtypeStruct((batch, seq, MOBA_W), BF16),
        scratch_shapes=[
            pltpu.VMEM((4 * nb, width), BF16),
            pltpu.VMEM((hps, 2, seq, MOBA_BLOCK), F32),
            pltpu.VMEM((hps, 2, seq, MOBA_BLOCK), BF16),
        ],
        compiler_params=pltpu.CompilerParams(
            dimension_semantics=("arbitrary", "arbitrary"), vmem_limit_bytes=V7X_VMEM_LIMIT_BYTES),
        name="block_attention",
    )(q_t, proj3, v_t, k_mean)


def _mix_ffn_kernel(x_ref, or_ref, om_ref, an_ref, wg32_ref, bg_ref, wro32_ref, wmo32_ref,
                    wo32_ref, fn_ref, wup32_ref, cw_ref, cb_ref, wdn32_ref, out_ref,
                    wg_ref, wro_ref, wmo_ref, wo_ref, wup_ref, wdn_ref,
                    h_s, merged_s, act_s, ubuf_s, carry_s, *, tiles_per_seq):
    step = pl.program_id(0)

    @pl.when(step < WEIGHT_CAST_STEPS)
    def _():
        for src, dst in ((wg32_ref, wg_ref), (wro32_ref, wro_ref), (wmo32_ref, wmo_ref),
                         (wo32_ref, wo_ref), (wup32_ref, wup_ref), (wdn32_ref, wdn_ref)):
            rows = src.shape[0]
            start = pl.multiple_of(step * rows, V7X_BF16_SUBLANE_TILE)
            dst[pl.ds(start, rows), :] = src[...].astype(BF16)

    @pl.when(step >= WEIGHT_CAST_STEPS)
    def _():
        _mix_ffn_tile(x_ref, or_ref, om_ref, an_ref, wg_ref, bg_ref, wro_ref, wmo_ref, wo_ref,
                      fn_ref, wup_ref, cw_ref, cb_ref, wdn_ref, out_ref,
                      h_s, merged_s, act_s, ubuf_s, carry_s,
                      tile=step - WEIGHT_CAST_STEPS, tiles_per_seq=tiles_per_seq)


def _mix_ffn_tile(x_ref, or_ref, om_ref, an_ref, wg_ref, bg_ref, wro_ref, wmo_ref, wo_ref,
                  fn_ref, wup_ref, cw_ref, cb_ref, wdn_ref, out_ref,
                  h_s, merged_s, act_s, ubuf_s, carry_s, *, tile, tiles_per_seq):
    tm = x_ref.shape[0]
    halo = V7X_SUBLANES

    @pl.when(tile % tiles_per_seq == 0)
    def _():
        carry_s[...] = jnp.zeros_like(carry_s)

    x = x_ref[...]
    h_s[...] = _rms(x, an_ref[...]).astype(BF16)
    for c in range(D_MODEL // MIX_COL_CHUNK):
        cs = slice(c * MIX_COL_CHUNK, (c + 1) * MIX_COL_CHUNK)
        cs2 = slice(D_MODEL + c * MIX_COL_CHUNK, D_MODEL + (c + 1) * MIX_COL_CHUNK)
        y_ret = jnp.dot(or_ref[...], wro_ref[:, cs], preferred_element_type=F32)
        y_moba = jnp.dot(om_ref[...], wmo_ref[:, cs], preferred_element_type=F32)
        g_ret = jax.nn.sigmoid(
            jnp.dot(h_s[...], wg_ref[:, cs], preferred_element_type=F32) + bg_ref[:, cs])
        g_moba = jax.nn.sigmoid(
            jnp.dot(h_s[...], wg_ref[:, cs2], preferred_element_type=F32) + bg_ref[:, cs2])
        merged_s[:, cs] = (g_ret * y_ret + g_moba * y_moba).astype(BF16)
    x1 = x + jnp.dot(merged_s[...], wo_ref[...], preferred_element_type=F32)
    out_ref[...] = x1
    h_s[...] = _rms(x1, fn_ref[...]).astype(BF16)

    def conv(u, slot, cols):
        ubuf_s[slot, 0:halo, :] = carry_s[:, cols]
        ubuf_s[slot, halo:halo + tm, :] = u
        carry_s[:, cols] = u[tm - halo:tm, :]
        u1 = ubuf_s[slot, halo - 1:halo - 1 + tm, :]
        u2 = ubuf_s[slot, halo - 2:halo - 2 + tm, :]
        return (cb_ref[:, cols] + cw_ref[0:1, cols] * u2 + cw_ref[1:2, cols] * u1
                + cw_ref[2:3, cols] * u)

    n_ff = D_FF // FF_CHUNK
    head = (n_ff - 1) * FF_CHUNK
    for c in range(n_ff):
        gcols = slice(c * FF_CHUNK, (c + 1) * FF_CHUNK)
        vcols = slice(D_FF + c * FF_CHUNK, D_FF + (c + 1) * FF_CHUNK)
        d_gate = jnp.dot(h_s[...], wup_ref[:, gcols], preferred_element_type=F32)
        d_val = jnp.dot(h_s[...], wup_ref[:, vcols], preferred_element_type=F32)
        if c == n_ff - 1:
            out_ref[...] += jnp.dot(act_s[:, 0:head], wdn_ref[0:head, :],
                                    preferred_element_type=F32)
        for p in range(FF_CHUNK // V7X_LANES):
            piece = slice(p * V7X_LANES, (p + 1) * V7X_LANES)
            g_piece = slice(gcols.start + piece.start, gcols.start + piece.stop)
            v_piece = slice(vcols.start + piece.start, vcols.start + piece.stop)
            u_gate = conv(d_gate[:, piece], 2 * p, g_piece)
            u_val = conv(d_val[:, piece], 2 * p + 1, v_piece)
            act_s[:, g_piece] = (u_gate * jax.nn.sigmoid(u_gate) * u_val).astype(BF16)
    out_ref[...] += jnp.dot(act_s[:, head:D_FF], wdn_ref[head:D_FF, :], preferred_element_type=F32)


def _mix_ffn(x2, o_r, o_m, attn_norm, w_gate, b_gate, w_ret_o, w_moba_o, w_out,
             ffn_norm, w_up, conv_w, conv_b, w_down, seq):
    tokens = x2.shape[0]
    n_cast = WEIGHT_CAST_STEPS
    tile = lambda cols: pl.BlockSpec(
        (TOKEN_TILE, cols), lambda i: (jnp.maximum(i - n_cast, 0), 0))
    chunk = lambda w: pl.BlockSpec(
        (w.shape[0] // n_cast, w.shape[1]), lambda i: (jnp.minimum(i, n_cast - 1), 0))
    weights = (w_gate, w_ret_o, w_moba_o, w_out, w_up, w_down)
    assert all(w.shape[0] % (n_cast * V7X_BF16_SUBLANE_TILE) == 0 for w in weights)
    return pl.pallas_call(
        functools.partial(_mix_ffn_kernel, tiles_per_seq=seq // TOKEN_TILE),
        grid=(n_cast + tokens // TOKEN_TILE,),
        in_specs=[
            tile(D_MODEL), tile(RET_V), tile(MOBA_W),
            _resident((1, D_MODEL)), chunk(w_gate), _resident((1, 2 * D_MODEL)),
            chunk(w_ret_o), chunk(w_moba_o), chunk(w_out),
            _resident((1, D_MODEL)), chunk(w_up),
            _resident((CONV_WIDTH, 2 * D_FF)), _resident((1, 2 * D_FF)), chunk(w_down),
        ],
        out_specs=tile(D_MODEL),
        out_shape=jax.ShapeDtypeStruct((tokens, D_MODEL), F32),
        scratch_shapes=[
            *[pltpu.VMEM(w.shape, BF16) for w in weights],
            pltpu.VMEM((TOKEN_TILE, D_MODEL), BF16),
            pltpu.VMEM((TOKEN_TILE, D_MODEL), BF16),
            pltpu.VMEM((TOKEN_TILE, D_FF), BF16),
            pltpu.VMEM((2 * FF_CHUNK // V7X_LANES, TOKEN_TILE + V7X_SUBLANES, V7X_LANES), F32),
            pltpu.VMEM((V7X_SUBLANES, 2 * D_FF), F32),
        ],
        compiler_params=pltpu.CompilerParams(
            dimension_semantics=("arbitrary",), vmem_limit_bytes=V7X_VMEM_LIMIT_BYTES),
        name="merge_ffn",
    )(x2, o_r, o_m, attn_norm, w_gate, b_gate, w_ret_o, w_moba_o, w_out,
      ffn_norm, w_up, conv_w, conv_b, w_down)


def _rotary_tables(seq):
    half = RET_QK_DIM // 2
    inv_freq = ROPE_BASE ** (-jnp.arange(half, dtype=F32) / half)
    ang = jnp.arange(seq).astype(F32)[:, None] * inv_freq[None, :]
    cos, sin = jnp.cos(ang), jnp.sin(ang)
    return jnp.concatenate([cos, cos], axis=-1), jnp.concatenate([-sin, sin], axis=-1)


def _retention_tables():
    log_gamma = jnp.log1p(-jnp.power(2.0, -5.0 - jnp.arange(RET_HEADS, dtype=F32)))
    pos = jnp.arange(RET_CHUNK, dtype=F32)
    diff = pos[:, None] - pos[None, :]
    decay = jnp.where(diff >= 0, jnp.exp(log_gamma[:, None, None] * jnp.maximum(diff, 0.0)), 0.0)
    zeta = jnp.exp(log_gamma[:, None] * (RET_CHUNK - 1 - pos)[None, :])
    xi = jnp.exp(log_gamma[:, None] * (pos + 1)[None, :])
    gamma_c = jnp.exp(log_gamma * RET_CHUNK)
    xi_b = jnp.broadcast_to(xi[:, :, None], (RET_HEADS, RET_CHUNK, RET_V_DIM))
    zeta_b = jnp.broadcast_to(zeta[:, :, None], (RET_HEADS, RET_CHUNK, RET_QK_DIM))
    gamma_b = jnp.broadcast_to(gamma_c[:, None, None], (RET_HEADS, 1, RET_V_DIM))
    return decay, xi_b, zeta_b, gamma_b


def kernel(x, attn_norm, w_in, ret_norm_gain, ret_norm_bias, moba_q_gain, moba_k_gain,
           w_ret_o, w_moba_o, w_gate, b_gate, w_out, ffn_norm, w_up, conv_w, conv_b, w_down):
    batch, seq, d_model = x.shape
    depth = attn_norm.shape[0]
    assert d_model == D_MODEL and seq % TOKEN_TILE == 0 and seq % MOBA_BLOCK == 0
    cos, sin = _rotary_tables(seq)
    decay, xi_b, zeta_b, gamma_b = _retention_tables()
    x2 = x.reshape(batch * seq, d_model)
    for l in range(depth):
        q_gain = moba_q_gain[l][None] * (MOBA_HEAD_DIM ** -0.5 * LOG2_E)
        proj, k_mean, v_t, q_t = _in_projection(x2, attn_norm[l][None], cos, sin, q_gain,
                                                moba_k_gain[l][None], w_in[l].astype(BF16), seq)
        proj3 = proj.reshape(batch, seq, PROJ_W)
        o_r = _retention(proj3, decay, xi_b, zeta_b, gamma_b,
                         ret_norm_gain[l][None], ret_norm_bias[l][None])
        o_m = _moba(proj3, k_mean, v_t, q_t)
        x2 = _mix_ffn(
            x2, o_r.reshape(batch * seq, RET_V), o_m.reshape(batch * seq, MOBA_W),
            attn_norm[l][None], w_gate[l], b_gate[l][None], w_ret_o[l], w_moba_o[l], w_out[l],
            ffn_norm[l][None], w_up[l], conv_w[l], conv_b[l][None], w_down[l], seq)
    return x2.reshape(batch, seq, d_model)
```

```python
import functools

import jax
import jax.numpy as jnp
from jax import lax
from jax.experimental import pallas as pl
from jax.experimental.pallas import tpu as pltpu

F32 = jnp.float32
BF16 = jnp.bfloat16

D_MODEL = 1024
RET_HEADS = 4
RET_QK_DIM = 128
RET_V_DIM = 256
ROPE_BASE = 10000.0
MOBA_HEADS = 8
MOBA_HEAD_DIM = 128
MOBA_BLOCK = 256
MOBA_TOPK = 3
D_FF = 2816
CONV_WIDTH = 3
NORM_EPS = 1e-6
LOG2_E = 1.4426950408889634

RET_QK = RET_HEADS * RET_QK_DIM
RET_V = RET_HEADS * RET_V_DIM
MOBA_W = MOBA_HEADS * MOBA_HEAD_DIM
D_IN = 2 * RET_QK + 2 * RET_V + 3 * MOBA_W
OFF_QR, OFF_KR, OFF_VR, OFF_GR = 0, RET_QK, 2 * RET_QK, 2 * RET_QK + RET_V
OFF_QM = 2 * RET_QK + 2 * RET_V
OFF_KM, OFF_VM = OFF_QM + MOBA_W, OFF_QM + 2 * MOBA_W
PROJ_OFF_KM = OFF_QM
PROJ_W = PROJ_OFF_KM + MOBA_W

V7X_LANES = 128
V7X_SUBLANES = 8
V7X_BF16_SUBLANE_TILE = 16
V7X_VMEM_LIMIT_BYTES = 56 * 1024 * 1024

TOKEN_TILE = 512
PROJ_COL_CHUNK = 512
MIX_COL_CHUNK = 512
WEIGHT_CAST_STEPS = 16
FF_CHUNK = 256
MOBA_HEADS_PER_STEP = 4
MOBA_VT_ROWS = MOBA_HEAD_DIM + 16
RET_HEADS_PER_STEP = 2
RET_CHUNK = 256


def _resident(shape):
    nd = len(shape)
    return pl.BlockSpec(shape, lambda *_: (0,) * nd, pipeline_mode=pl.Buffered(1))


def _rms(x, gain):
    inv = lax.rsqrt(jnp.mean(x * x, axis=-1, keepdims=True) + NORM_EPS)
    return (x * inv) * gain


def _inproj_kernel(x_ref, gain_ref, cos_ref, sin_ref, gq_ref, gk_ref, w32_ref,
                   o_ref, km_ref, vt_ref, qt_ref, w_ref, h_s):
    step = pl.program_id(0)

    @pl.when(step < WEIGHT_CAST_STEPS)
    def _():
        rows = w32_ref.shape[0]
        start = pl.multiple_of(step * rows, V7X_BF16_SUBLANE_TILE)
        w_ref[pl.ds(start, rows), :] = w32_ref[...].astype(BF16)

    @pl.when(step >= WEIGHT_CAST_STEPS)
    def _():
        _inproj_tile(x_ref, gain_ref, cos_ref, sin_ref, gq_ref, gk_ref, w_ref,
                     o_ref, km_ref, vt_ref, qt_ref, h_s)


def _inproj_tile(x_ref, gain_ref, cos_ref, sin_ref, gq_ref, gk_ref, w_ref,
                 o_ref, km_ref, vt_ref, qt_ref, h_s):
    tm = x_ref.shape[0]
    h_s[...] = _rms(x_ref[...], gain_ref[...]).astype(BF16)
    cos = cos_ref[...]
    sin = sin_ref[...]
    half = RET_QK_DIM // 2
    starts = list(range(0, D_IN, PROJ_COL_CHUNK))
    order = ([c0 for c0 in starts if c0 >= OFF_QM][::-1]
             + [c0 for c0 in starts if c0 < OFF_VR]
             + [c0 for c0 in starts if OFF_GR <= c0 < OFF_QM]
             + [c0 for c0 in starts if OFF_VR <= c0 < OFF_GR])
    for c0 in order:
        acc = jnp.dot(h_s[...], w_ref[:, c0:c0 + PROJ_COL_CHUNK], preferred_element_type=F32)
        if c0 < OFF_VR:
            k_scale = RET_QK_DIM ** -0.5 if c0 >= OFF_KR else None
            for s in range(PROJ_COL_CHUNK // RET_QK_DIM):
                seg = acc[:, s * RET_QK_DIM:(s + 1) * RET_QK_DIM]
                seg = seg * cos + pltpu.roll(seg, half, axis=1) * sin
                if k_scale is not None:
                    seg = seg * k_scale
                o_ref[:, c0 + s * RET_QK_DIM:c0 + (s + 1) * RET_QK_DIM] = seg.astype(BF16)
        elif OFF_QM <= c0 < OFF_VM:
            is_key = c0 >= OFF_KM
            gain = gk_ref[...] if is_key else gq_ref[...]
            for s in range(PROJ_COL_CHUNK // MOBA_HEAD_DIM):
                seg = _rms(acc[:, s * MOBA_HEAD_DIM:(s + 1) * MOBA_HEAD_DIM], gain)
                if is_key:
                    head = (c0 - OFF_KM) // MOBA_HEAD_DIM + s
                    kcols = slice(head * MOBA_HEAD_DIM, (head + 1) * MOBA_HEAD_DIM)
                    o_ref[:, PROJ_OFF_KM + kcols.start:PROJ_OFF_KM + kcols.stop] = seg.astype(BF16)
                    km_ref[:, kcols] = jnp.mean(
                        seg.reshape(tm // MOBA_BLOCK, MOBA_BLOCK, MOBA_HEAD_DIM), axis=1)
                else:
                    head = (c0 - OFF_QM) // MOBA_HEAD_DIM + s
                    qt_ref[head, :, :] = seg.T.astype(BF16)
        elif c0 >= OFF_VM:
            for s in range(PROJ_COL_CHUNK // MOBA_HEAD_DIM):
                head = (c0 - OFF_VM) // MOBA_HEAD_DIM + s
                seg = acc[:, s * MOBA_HEAD_DIM:(s + 1) * MOBA_HEAD_DIM]
                vt_ref[head, 0:MOBA_HEAD_DIM, :] = seg.T.astype(BF16)
                vt_ref[head, MOBA_HEAD_DIM:MOBA_VT_ROWS, :] = jnp.ones(
                    (MOBA_VT_ROWS - MOBA_HEAD_DIM, tm), BF16)
        else:
            o_ref[:, c0:c0 + PROJ_COL_CHUNK] = acc.astype(BF16)


def _in_projection(x2, attn_norm, cos, sin, q_gain, k_gain, w_in, seq):
    tokens = x2.shape[0]
    tiles_per_seq = seq // TOKEN_TILE
    blocks_per_tile = TOKEN_TILE // MOBA_BLOCK
    n_cast = WEIGHT_CAST_STEPS
    assert D_MODEL % (n_cast * V7X_BF16_SUBLANE_TILE) == 0
    tile = lambda i: jnp.maximum(i - n_cast, 0)
    proj, k_mean, v_t, q_t = pl.pallas_call(
        _inproj_kernel,
        grid=(n_cast + tokens // TOKEN_TILE,),
        in_specs=[
            pl.BlockSpec((TOKEN_TILE, D_MODEL), lambda i: (tile(i), 0)),
            _resident((1, D_MODEL)),
            pl.BlockSpec((TOKEN_TILE, RET_QK_DIM), lambda i: (tile(i) % tiles_per_seq, 0)),
            pl.BlockSpec((TOKEN_TILE, RET_QK_DIM), lambda i: (tile(i) % tiles_per_seq, 0)),
            _resident((1, MOBA_HEAD_DIM)),
            _resident((1, MOBA_HEAD_DIM)),
            pl.BlockSpec((D_MODEL // n_cast, D_IN), lambda i: (jnp.minimum(i, n_cast - 1), 0)),
        ],
        out_specs=[
            pl.BlockSpec((TOKEN_TILE, PROJ_W), lambda i: (tile(i), 0)),
            pl.BlockSpec((None, blocks_per_tile, MOBA_W), lambda i: (tile(i), 0, 0)),
            pl.BlockSpec((MOBA_HEADS, MOBA_VT_ROWS, TOKEN_TILE), lambda i: (0, 0, tile(i))),
            pl.BlockSpec((MOBA_HEADS, MOBA_HEAD_DIM, TOKEN_TILE), lambda i: (0, 0, tile(i))),
        ],
        out_shape=[
            jax.ShapeDtypeStruct((tokens, PROJ_W), BF16),
            jax.ShapeDtypeStruct((tokens // TOKEN_TILE, blocks_per_tile, MOBA_W), F32),
            jax.ShapeDtypeStruct((MOBA_HEADS, MOBA_VT_ROWS, tokens), BF16),
            jax.ShapeDtypeStruct((MOBA_HEADS, MOBA_HEAD_DIM, tokens), BF16),
        ],
        scratch_shapes=[pltpu.VMEM((D_MODEL, D_IN), BF16),
                        pltpu.VMEM((TOKEN_TILE, D_MODEL), BF16)],
        compiler_params=pltpu.CompilerParams(
            dimension_semantics=("arbitrary",), vmem_limit_bytes=V7X_VMEM_LIMIT_BYTES),
        name="in_projection",
    )(x2, attn_norm, cos, sin, q_gain, k_gain, w_in)
    return proj, k_mean.reshape(tokens // MOBA_BLOCK, MOBA_W), v_t, q_t


def _retention_kernel(q_ref, k_ref, v_ref, g_ref, dec_ref, xi_ref, zeta_ref, gc_ref,
                      gain_ref, bias_ref, o_ref):
    seq = q_ref.shape[0]
    n_heads = dec_ref.shape[0]
    state = [jnp.zeros((RET_QK_DIM, RET_V_DIM), F32) for _ in range(n_heads)]
    for c in range(seq // RET_CHUNK):
        rows = slice(c * RET_CHUNK, (c + 1) * RET_CHUNK)
        for h in range(n_heads):
            qk_cols = slice(h * RET_QK_DIM, (h + 1) * RET_QK_DIM)
            v_cols = slice(h * RET_V_DIM, (h + 1) * RET_V_DIM)
            qc, kc, vc = q_ref[rows, qk_cols], k_ref[rows, qk_cols], v_ref[rows, v_cols]
            scores = lax.dot_general(qc, kc, (((1,), (1,)), ((), ())),
                                     preferred_element_type=F32) * dec_ref[h]
            o = jnp.dot(scores.astype(BF16), vc, preferred_element_type=F32)
            o = o + jnp.dot(qc, state[h].astype(BF16), preferred_element_type=F32) * xi_ref[h]
            kz = (kc.astype(F32) * zeta_ref[h]).astype(BF16)
            state[h] = gc_ref[h] * state[h] + lax.dot_general(
                kz, vc, (((0,), (0,)), ((), ())), preferred_element_type=F32)
            mu = jnp.mean(o, axis=-1, keepdims=True)
            d = o - mu
            var = jnp.mean(d * d, axis=-1, keepdims=True)
            on = d * lax.rsqrt(var + NORM_EPS)
            g = g_ref[rows, v_cols].astype(F32)
            o_ref[rows, v_cols] = ((on * gain_ref[:, v_cols] + bias_ref[:, v_cols])
                                   * (g * jax.nn.sigmoid(g))).astype(BF16)


def _retention(proj3, decay, xi, zeta, gamma_c, gain, bias):
    batch, seq, _ = proj3.shape
    hps = RET_HEADS_PER_STEP
    qk_w, v_w = hps * RET_QK_DIM, hps * RET_V_DIM
    qk_blk = lambda off: pl.BlockSpec((None, seq, qk_w), lambda b, g: (b, 0, off // qk_w + g))
    v_blk = lambda off: pl.BlockSpec((None, seq, v_w), lambda b, g: (b, 0, off // v_w + g))
    per_head = lambda r, c: pl.BlockSpec((hps, r, c), lambda b, g: (g, 0, 0))
    return pl.pallas_call(
        _retention_kernel,
        grid=(batch, RET_HEADS // hps),
        in_specs=[
            qk_blk(OFF_QR), qk_blk(OFF_KR), v_blk(OFF_VR), v_blk(OFF_GR),
            per_head(RET_CHUNK, RET_CHUNK), per_head(RET_CHUNK, RET_V_DIM),
            per_head(RET_CHUNK, RET_QK_DIM), per_head(1, RET_V_DIM),
            pl.BlockSpec((1, v_w), lambda b, g: (0, g)),
            pl.BlockSpec((1, v_w), lambda b, g: (0, g)),
        ],
        out_specs=pl.BlockSpec((None, seq, v_w), lambda b, g: (b, 0, g)),
        out_shape=jax.ShapeDtypeStruct((batch, seq, RET_V), BF16),
        compiler_params=pltpu.CompilerParams(
            dimension_semantics=("arbitrary", "arbitrary"), vmem_limit_bytes=V7X_VMEM_LIMIT_BYTES),
        name="retention",
    )(proj3, proj3, proj3, proj3, decay, xi, zeta, gamma_c, gain, bias)


def _moba_kernel(qt_ref, k_ref, vt_ref, km_ref, o_ref, km3_s, s_s, p_s):
    seq = k_ref.shape[0]
    n_heads = vt_ref.shape[0]
    nb = seq // MOBA_BLOCK
    km = km_ref[...]
    km_hi = km.astype(BF16)
    km_mid = (km - km_hi.astype(F32)).astype(BF16)
    km_lo = (km - km_hi.astype(F32) - km_mid.astype(F32)).astype(BF16)
    km3_s[...] = jnp.concatenate([km_hi, km_mid, km_lo, jnp.zeros_like(km_hi)], axis=0)

    key_i = lax.broadcasted_iota(jnp.int32, (MOBA_BLOCK, MOBA_BLOCK), 0)
    qry_i = lax.broadcasted_iota(jnp.int32, (MOBA_BLOCK, MOBA_BLOCK), 1)
    causal = key_i <= qry_i

    def blk(j):
        return slice(j * MOBA_BLOCK, (j + 1) * MOBA_BLOCK)

    def hcols(h):
        return slice(h * MOBA_HEAD_DIM, (h + 1) * MOBA_HEAD_DIM)

    def keep_masks(h, qb):
        if qb <= MOBA_TOPK:
            return [None] * qb
        parts = jnp.dot(km3_s[:, hcols(h)], qt_ref[h, :, blk(qb)],
                        preferred_element_type=F32)
        gate = parts[0:nb] + (parts[nb:2 * nb] + parts[2 * nb:3 * nb])
        g = [gate[i:i + 1, :] for i in range(qb)]
        keep = []
        for j in range(qb):
            beaten = jnp.zeros((1, MOBA_BLOCK), jnp.int32)
            for i in range(qb):
                if i < j:
                    beaten = beaten + (g[i] >= g[j]).astype(jnp.int32)
                elif i > j:
                    beaten = beaten + (g[i] > g[j]).astype(jnp.int32)
            keep.append(beaten < MOBA_TOPK)
        return keep

    def scores_stage(h, qb):
        slot = qb % 2
        keep = keep_masks(h, qb)
        q_blk_t = qt_ref[h, :, blk(qb)]
        m = None
        for j in range(qb + 1):
            s = jnp.dot(k_ref[blk(j), hcols(h)], q_blk_t,
                        preferred_element_type=F32)
            if j == qb:
                s = jnp.where(causal, s, -jnp.inf)
            elif keep[j] is not None:
                s = jnp.where(keep[j], s, -jnp.inf)
            s_s[h, slot, blk(j), :] = s
            mj = s_s[h, slot, blk(j), :].max(axis=0, keepdims=True)
            m = mj if m is None else jnp.maximum(m, mj)
        return m

    def softmax_stage(h, qb, m):
        slot = qb % 2
        for j in range(qb + 1):
            p_s[h, slot, blk(j), :] = jnp.exp2(s_s[h, slot, blk(j), :] - m).astype(BF16)

    def value_stage(h, qb):
        slot = qb % 2
        nk = (qb + 1) * MOBA_BLOCK
        o_t = jnp.dot(vt_ref[h, :, 0:nk], p_s[h, slot, 0:nk, :],
                      preferred_element_type=F32)
        denom = o_t[MOBA_HEAD_DIM:MOBA_HEAD_DIM + 1, :]
        o_ref[blk(qb), hcols(h)] = (o_t[0:MOBA_HEAD_DIM, :] / denom).T.astype(BF16)

    heads = range(n_heads)
    m_next = [scores_stage(h, 0) for h in heads]
    for qb in range(nb):
        m_cur = m_next
        if qb + 1 < nb:
            m_next = [scores_stage(h, qb + 1) for h in heads]
        for h in heads:
            softmax_stage(h, qb, m_cur[h])
            value_stage(h, qb)


def _moba(proj3, k_mean, v_t, q_t):
    batch, seq, _ = proj3.shape
    nb = seq // MOBA_BLOCK
    hps = MOBA_HEADS_PER_STEP
    width = hps * MOBA_HEAD_DIM
    return pl.pallas_call(
        _moba_kernel,
        grid=(batch, MOBA_HEADS // hps),
        in_specs=[
            pl.BlockSpec((hps, MOBA_HEAD_DIM, seq), lambda b, g: (g, 0, b)),
            pl.BlockSpec((None, seq, width), lambda b, g: (b, 0, PROJ_OFF_KM // width + g)),
            pl.BlockSpec((hps, MOBA_VT_ROWS, seq), lambda b, g: (g, 0, b)),
            pl.BlockSpec((nb, width), lambda b, g: (b, g)),
        ],
        out_specs=pl.BlockSpec((None, seq, width), lambda b, g: (b, 0, g)),
        out_shape=jax.ShapeDtypeStruct((batch, seq, MOBA_W), BF16),
        scratch_shapes=[
            pltpu.VMEM((4 * nb, width), BF16),
            pltpu.VMEM((hps, 2, seq, MOBA_BLOCK), F32),
            pltpu.VMEM((hps, 2, seq, MOBA_BLOCK), BF16),
        ],
        compiler_params=pltpu.CompilerParams(
            dimension_semantics=("arbitrary", "arbitrary"), vmem_limit_bytes=V7X_VMEM_LIMIT_BYTES),
        name="block_attention",
    )(q_t, proj3, v_t, k_mean)


def _mix_ffn_kernel(x_ref, or_ref, om_ref, an_ref, wg32_ref, bg_ref, wro32_ref, wmo32_ref,
                    wo32_ref, fn_ref, wup32_ref, cw_ref, cb_ref, wdn32_ref, out_ref,
                    wg_ref, wro_ref, wmo_ref, wo_ref, wup_ref, wdn_ref,
                    h_s, merged_s, act_s, ubuf_s, carry_s, *, tiles_per_seq):
    step = pl.program_id(0)

    @pl.when(step < WEIGHT_CAST_STEPS)
    def _():
        for src, dst in ((wg32_ref, wg_ref), (wro32_ref, wro_ref), (wmo32_ref, wmo_ref),
                         (wo32_ref, wo_ref), (wup32_ref, wup_ref), (wdn32_ref, wdn_ref)):
            rows = src.shape[0]
            start = pl.multiple_of(step * rows, V7X_BF16_SUBLANE_TILE)
            dst[pl.ds(start, rows), :] = src[...].astype(BF16)

    @pl.when(step >= WEIGHT_CAST_STEPS)
    def _():
        _mix_ffn_tile(x_ref, or_ref, om_ref, an_ref, wg_ref, bg_ref, wro_ref, wmo_ref, wo_ref,
                      fn_ref, wup_ref, cw_ref, cb_ref, wdn_ref, out_ref,
                      h_s, merged_s, act_s, ubuf_s, carry_s,
                      tile=step - WEIGHT_CAST_STEPS, tiles_per_seq=tiles_per_seq)


def _mix_ffn_tile(x_ref, or_ref, om_ref, an_ref, wg_ref, bg_ref, wro_ref, wmo_ref, wo_ref,
                  fn_ref, wup_ref, cw_ref, cb_ref, wdn_ref, out_ref,
                  h_s, merged_s, act_s, ubuf_s, carry_s, *, tile, tiles_per_seq):
    tm = x_ref.shape[0]
    halo = V7X_SUBLANES

    @pl.when(tile % tiles_per_seq == 0)
    def _():
        carry_s[...] = jnp.zeros_like(carry_s)

    x = x_ref[...]
    h_s[...] = _rms(x, an_ref[...]).astype(BF16)
    for c in range(D_MODEL // MIX_COL_CHUNK):
        cs = slice(c * MIX_COL_CHUNK, (c + 1) * MIX_COL_CHUNK)
        cs2 = slice(D_MODEL + c * MIX_COL_CHUNK, D_MODEL + (c + 1) * MIX_COL_CHUNK)
        y_ret = jnp.dot(or_ref[...], wro_ref[:, cs], preferred_element_type=F32)
        y_moba = jnp.dot(om_ref[...], wmo_ref[:, cs], preferred_element_type=F32)
        g_ret = jax.nn.sigmoid(
            jnp.dot(h_s[...], wg_ref[:, cs], preferred_element_type=F32) + bg_ref[:, cs])
        g_moba = jax.nn.sigmoid(
            jnp.dot(h_s[...], wg_ref[:, cs2], preferred_element_type=F32) + bg_ref[:, cs2])
        merged_s[:, cs] = (g_ret * y_ret + g_moba * y_moba).astype(BF16)
    x1 = x + jnp.dot(merged_s[...], wo_ref[...], preferred_element_type=F32)
    out_ref[...] = x1
    h_s[...] = _rms(x1, fn_ref[...]).astype(BF16)

    def conv(u, slot, cols):
        ubuf_s[slot, 0:halo, :] = carry_s[:, cols]
        ubuf_s[slot, halo:halo + tm, :] = u
        carry_s[:, cols] = u[tm - halo:tm, :]
        u1 = ubuf_s[slot, halo - 1:halo - 1 + tm, :]
        u2 = ubuf_s[slot, halo - 2:halo - 2 + tm, :]
        return (cb_ref[:, cols] + cw_ref[0:1, cols] * u2 + cw_ref[1:2, cols] * u1
                + cw_ref[2:3, cols] * u)

    n_ff = D_FF // FF_CHUNK
    head = (n_ff - 1) * FF_CHUNK
    for c in range(n_ff):
        gcols = slice(c * FF_CHUNK, (c + 1) * FF_CHUNK)
        vcols = slice(D_FF + c * FF_CHUNK, D_FF + (c + 1) * FF_CHUNK)
        d_gate = jnp.dot(h_s[...], wup_ref[:, gcols], preferred_element_type=F32)
        d_val = jnp.dot(h_s[...], wup_ref[:, vcols], preferred_element_type=F32)
        if c == n_ff - 1:
            out_ref[...] += jnp.dot(act_s[:, 0:head], wdn_ref[0:head, :],
                                    preferred_element_type=F32)
        for p in range(FF_CHUNK // V7X_LANES):
            piece = slice(p * V7X_LANES, (p + 1) * V7X_LANES)
            g_piece = slice(gcols.start + piece.start, gcols.start + piece.stop)
            v_piece = slice(vcols.start + piece.start, vcols.start + piece.stop)
            u_gate = conv(d_gate[:, piece], 2 * p, g_piece)
            u_val = conv(d_val[:, piece], 2 * p + 1, v_piece)
            act_s[:, g_piece] = (u_gate * jax.nn.sigmoid(u_gate) * u_val).astype(BF16)
    out_ref[...] += jnp.dot(act_s[:, head:D_FF], wdn_ref[head:D_FF, :], preferred_element_type=F32)


def _mix_ffn(x2, o_r, o_m, attn_norm, w_gate, b_gate, w_ret_o, w_moba_o, w_out,
             ffn_norm, w_up, conv_w, conv_b, w_down, seq):
    tokens = x2.shape[0]
    n_cast = WEIGHT_CAST_STEPS
    tile = lambda cols: pl.BlockSpec(
        (TOKEN_TILE, cols), lambda i: (jnp.maximum(i - n_cast, 0), 0))
    chunk = lambda w: pl.BlockSpec(
        (w.shape[0] // n_cast, w.shape[1]), lambda i: (jnp.minimum(i, n_cast - 1), 0))
    weights = (w_gate, w_ret_o, w_moba_o, w_out, w_up, w_down)
    assert all(w.shape[0] % (n_cast * V7X_BF16_SUBLANE_TILE) == 0 for w in weights)
    return pl.pallas_call(
        functools.partial(_mix_ffn_kernel, tiles_per_seq=seq // TOKEN_TILE),
        grid=(n_cast + tokens // TOKEN_TILE,),
        in_specs=[
            tile(D_MODEL), tile(RET_V), tile(MOBA_W),
            _resident((1, D_MODEL)), chunk(w_gate), _resident((1, 2 * D_MODEL)),
            chunk(w_ret_o), chunk(w_moba_o), chunk(w_out),
            _resident((1, D_MODEL)), chunk(w_up),
            _resident((CONV_WIDTH, 2 * D_FF)), _resident((1, 2 * D_FF)), chunk(w_down),
        ],
        out_specs=tile(D_MODEL),
        out_shape=jax.ShapeDtypeStruct((tokens, D_MODEL), F32),
        scratch_shapes=[
            *[pltpu.VMEM(w.shape, BF16) for w in weights],
            pltpu.VMEM((TOKEN_TILE, D_MODEL), BF16),
            pltpu.VMEM((TOKEN_TILE, D_MODEL), BF16),
            pltpu.VMEM((TOKEN_TILE, D_FF), BF16),
            pltpu.VMEM((2 * FF_CHUNK // V7X_LANES, TOKEN_TILE + V7X_SUBLANES, V7X_LANES), F32),
            pltpu.VMEM((V7X_SUBLANES, 2 * D_FF), F32),
        ],
        compiler_params=pltpu.CompilerParams(
            dimension_semantics=("arbitrary",), vmem_limit_bytes=V7X_VMEM_LIMIT_BYTES),
        name="merge_ffn",
    )(x2, o_r, o_m, attn_norm, w_gate, b_gate, w_ret_o, w_moba_o, w_out,
      ffn_norm, w_up, conv_w, conv_b, w_down)


def _rotary_tables(seq):
    half = RET_QK_DIM // 2
    inv_freq = ROPE_BASE ** (-jnp.arange(half, dtype=F32) / half)
    ang = jnp.arange(seq).astype(F32)[:, None] * inv_freq[None, :]
    cos, sin = jnp.cos(ang), jnp.sin(ang)
    return jnp.concatenate([cos, cos], axis=-1), jnp.concatenate([-sin, sin], axis=-1)


def _retention_tables():
    log_gamma = jnp.log1p(-jnp.power(2.0, -5.0 - jnp.arange(RET_HEADS, dtype=F32)))
    pos = jnp.arange(RET_CHUNK, dtype=F32)
    diff = pos[:, None] - pos[None, :]
    decay = jnp.where(diff >= 0, jnp.exp(log_gamma[:, None, None] * jnp.maximum(diff, 0.0)), 0.0)
    zeta = jnp.exp(log_gamma[:, None] * (RET_CHUNK - 1 - pos)[None, :])
    xi = jnp.exp(log_gamma[:, None] * (pos + 1)[None, :])
    gamma_c = jnp.exp(log_gamma * RET_CHUNK)
    xi_b = jnp.broadcast_to(xi[:, :, None], (RET_HEADS, RET_CHUNK, RET_V_DIM))
    zeta_b = jnp.broadcast_to(zeta[:, :, None], (RET_HEADS, RET_CHUNK, RET_QK_DIM))
    gamma_b = jnp.broadcast_to(gamma_c[:, None, None], (RET_HEADS, 1, RET_V_DIM))
    return decay, xi_b, zeta_b, gamma_b


def kernel(x, attn_norm, w_in, ret_norm_gain, ret_norm_bias, moba_q_gain, moba_k_gain,
           w_ret_o, w_moba_o, w_gate, b_gate, w_out, ffn_norm, w_up, conv_w, conv_b, w_down):
    batch, seq, d_model = x.shape
    depth = attn_norm.shape[0]
    assert d_model == D_MODEL and seq % TOKEN_TILE == 0 and seq % MOBA_BLOCK == 0
    cos, sin = _rotary_tables(seq)
    decay, xi_b, zeta_b, gamma_b = _retention_tables()
    x2 = x.reshape(batch * seq, d_model)
    for l in range(depth):
        q_gain = moba_q_gain[l][None] * (MOBA_HEAD_DIM ** -0.5 * LOG2_E)
        proj, k_mean, v_t, q_t = _in_projection(x2, attn_norm[l][None], cos, sin, q_gain,
                                                moba_k_gain[l][None], w_in[l], seq)
        proj3 = proj.reshape(batch, seq, PROJ_W)
        o_r = _retention(proj3, decay, xi_b, zeta_b, gamma_b,
                         ret_norm_gain[l][None], ret_norm_bias[l][None])
        o_m = _moba(proj3, k_mean, v_t, q_t)
        x2 = _mix_ffn(
            x2, o_r.reshape(batch * seq, RET_V), o_m.reshape(batch * seq, MOBA_W),
            attn_norm[l][None], w_gate[l], b_gate[l][None], w_ret_o[l], w_moba_o[l], w_out[l],
            ffn_norm[l][None], w_up[l], conv_w[l], conv_b[l][None], w_down[l], seq)
    return x2.reshape(batch, seq, d_model)
```
